```python
import jax, jax.numpy as jnp
from jax import lax
import numpy as np

D_MODEL = 2048
BATCH = 2
SEQ = 4096
DEPTH = 4

CHUNK = 64
EPS = 1e-6
D_FF = ((8 * D_MODEL // 3 + 127) // 128) * 128
D_CONV = D_MODEL // 2
CONV_WIDTH = 31
POOL_WINDOWS = (2, 4, 8, 16)
N_POOL_GROUPS = len(POOL_WINDOWS)
D_POOL = D_MODEL // 2
POOL_GROUP = D_POOL // N_POOL_GROUPS
D_AB_IN = 2 * D_CONV + D_POOL
D_AB_OUT = D_CONV + D_POOL
SGU_LEN = 128
D_SGU = D_MODEL
SGU_HEADS = 8
SGU_HEAD_DIM = D_SGU // SGU_HEADS
N_EVEN = (DEPTH + 1) // 2
N_ODD = DEPTH // 2

kernel_name = 'hybrid_conv_pool_sgu_macaron_trunk'


def _rms_norm(x, g):
    xf = x.astype(jnp.float32)
    y = xf * lax.rsqrt(jnp.mean(xf * xf, axis=-1, keepdims=True) + EPS)
    return (y * g.astype(jnp.float32)).astype(x.dtype)


def _layer_norm(x, g, b):
    xf = x.astype(jnp.float32)
    mu = jnp.mean(xf, axis=-1, keepdims=True)
    xc = xf - mu
    var = jnp.mean(xc * xc, axis=-1, keepdims=True)
    y = xc * lax.rsqrt(var + EPS) * g.astype(jnp.float32) + b.astype(jnp.float32)
    return y.astype(x.dtype)


def _swiglu(h, w_gate, w_up, w_down):
    return (jax.nn.silu(h @ w_gate) * (h @ w_up)) @ w_down


def _causal_depthwise_conv(a, w, b):
    k = w.shape[0]
    y = lax.conv_general_dilated(
        a, w[:, None, :].astype(a.dtype), window_strides=(1,), padding=[(k - 1, 0)],
        dimension_numbers=('NWC', 'WIO', 'NWC'), feature_group_count=a.shape[-1])
    return y + b


def _multiscale_pool(p):
    bsz, s, _ = p.shape
    pg = p.reshape(bsz, s, N_POOL_GROUPS, POOL_GROUP).astype(jnp.float32)
    cs = jnp.cumsum(pg, axis=1)
    counts = jnp.arange(1, s + 1, dtype=jnp.float32)
    outs = []
    for g, w in enumerate(POOL_WINDOWS):
        c = cs[:, :, g]
        lagged = jnp.pad(c, ((0, 0), (w, 0), (0, 0)))[:, :s]
        cnt = jnp.minimum(counts, jnp.float32(w))[None, :, None]
        outs.append((c - lagged) / cnt)
    mean = jnp.stack(outs, axis=2)
    return (mean - pg).astype(p.dtype)


def _conv_pool_mixer(h, w_in, conv_w, conv_b, ln_g, ln_b, pool_w, pool_scale, w_out):
    z = h @ w_in
    a, gate, p = jnp.split(z, [D_CONV, 2 * D_CONV], axis=-1)
    a = a * jax.nn.sigmoid(gate)
    a = _causal_depthwise_conv(a, conv_w, conv_b)
    a = jax.nn.silu(_layer_norm(a, ln_g, ln_b))
    pooled = _multiscale_pool(p)
    pb = jnp.einsum('bsgc,gcd->bsgd', pooled, pool_w).reshape(p.shape) * pool_scale
    return jnp.concatenate([a, pb], axis=-1) @ w_out


def _sgu_mixer(h, w_in, ln_g, ln_b, sgu_w, sgu_b, w_out):
    bsz, s, _ = h.shape
    z = jax.nn.gelu(h @ w_in)
    u, v = jnp.split(z, 2, axis=-1)
    v = _layer_norm(v, ln_g, ln_b)
    v = v.reshape(bsz, s // SGU_LEN, SGU_LEN, SGU_HEADS, SGU_HEAD_DIM)
    pos = jnp.arange(SGU_LEN)
    mask = (pos[None, :] // CHUNK) <= (pos[:, None] // CHUNK)
    w = jnp.where(mask[None], sgu_w, 0)
    mixed = jnp.einsum('gij,bnjgc->bnigc', w, v) + sgu_b.T[None, None, :, :, None]
    return (u * mixed.reshape(bsz, s, D_SGU)) @ w_out


def setup_inputs(seed: int = 0) -> dict:
    key = jax.random.key(seed)
    ks = iter(jax.random.split(key, 32))

    def nrm(shape, scale):
        return jax.random.normal(next(ks), shape, jnp.float32) * scale

    def gain(shape):
        return 1.0 + 0.1 * jax.random.normal(next(ks), shape, jnp.float32)

    return {
        'x': nrm((BATCH, SEQ, D_MODEL), 1.0),
        'norm_ffn1': gain((DEPTH, D_MODEL)),
        'ffn1_w_gate': nrm((DEPTH, D_MODEL, D_FF), D_MODEL ** -0.5),
        'ffn1_w_up': nrm((DEPTH, D_MODEL, D_FF), D_MODEL ** -0.5),
        'ffn1_w_down': nrm((DEPTH, D_FF, D_MODEL), D_FF ** -0.5),
        'norm_mix': gain((DEPTH, D_MODEL)),
        'norm_ffn2': gain((DEPTH, D_MODEL)),
        'ffn2_w_gate': nrm((DEPTH, D_MODEL, D_FF), D_MODEL ** -0.5),
        'ffn2_w_up': nrm((DEPTH, D_MODEL, D_FF), D_MODEL ** -0.5),
        'ffn2_w_down': nrm((DEPTH, D_FF, D_MODEL), D_FF ** -0.5),
        'ab_w_in': nrm((N_EVEN, D_MODEL, D_AB_IN), D_MODEL ** -0.5),
        'conv_w': nrm((N_EVEN, CONV_WIDTH, D_CONV), CONV_WIDTH ** -0.5),
        'conv_b': nrm((N_EVEN, D_CONV), 0.02),
        'conv_ln_g': gain((N_EVEN, D_CONV)),
        'conv_ln_b': nrm((N_EVEN, D_CONV), 0.02),
        'pool_w': nrm((N_EVEN, N_POOL_GROUPS, POOL_GROUP, POOL_GROUP), POOL_GROUP ** -0.5),
        'pool_scale': gain((N_EVEN, D_POOL)),
        'ab_w_out': nrm((N_EVEN, D_AB_OUT, D_MODEL), D_AB_OUT ** -0.5),
        'c_w_in': nrm((N_ODD, D_MODEL, 2 * D_SGU), D_MODEL ** -0.5),
        'sgu_ln_g': gain((N_ODD, D_SGU)),
        'sgu_ln_b': nrm((N_ODD, D_SGU), 0.02),
        'sgu_w': nrm((N_ODD, SGU_HEADS, SGU_LEN, SGU_LEN), 0.5 * SGU_LEN ** -0.5),
        'sgu_b': gain((N_ODD, SGU_HEADS, SGU_LEN)),
        'c_w_out': nrm((N_ODD, D_SGU, D_MODEL), D_SGU ** -0.5),
        'final_norm': gain((D_MODEL,)),
    }


def reference(x, norm_ffn1, ffn1_w_gate, ffn1_w_up, ffn1_w_down, norm_mix,
              norm_ffn2, ffn2_w_gate, ffn2_w_up, ffn2_w_down,
              ab_w_in, conv_w, conv_b, conv_ln_g, conv_ln_b, pool_w, pool_scale, ab_w_out,
              c_w_in, sgu_ln_g, sgu_ln_b, sgu_w, sgu_b, c_w_out, final_norm):
    for layer in range(DEPTH):
        x = x + 0.5 * _swiglu(_rms_norm(x, norm_ffn1[layer]),
                              ffn1_w_gate[layer], ffn1_w_up[layer], ffn1_w_down[layer])
        h = _rms_norm(x, norm_mix[layer])
        i = layer // 2
        if layer % 2 == 0:
            x = x + _conv_pool_mixer(h, ab_w_in[i], conv_w[i], conv_b[i], conv_ln_g[i],
                                     conv_ln_b[i], pool_w[i], pool_scale[i], ab_w_out[i])
        else:
            x = x + _sgu_mixer(h, c_w_in[i], sgu_ln_g[i], sgu_ln_b[i], sgu_w[i],
                               sgu_b[i], c_w_out[i])
        x = x + 0.5 * _swiglu(_rms_norm(x, norm_ffn2[layer]),
                              ffn2_w_gate[layer], ffn2_w_up[layer], ffn2_w_down[layer])
    return _rms_norm(x, final_norm)
```

```python
import functools

import jax
import jax.numpy as jnp
from jax import lax
from jax.experimental import pallas as pl
from jax.experimental.pallas import tpu as pltpu

EPS = 1e-6
CHUNK = 64
CONV_WIDTH = 31
POOL_WINDOWS = (2, 4, 8, 16)
SGU_LEN = 128
SGU_HEADS = 8
HALO = 32
LANES = 128
BF16 = jnp.bfloat16
F32 = jnp.float32

VMEM_LIMIT_BYTES = 58 * 1024 * 1024


def _cparams(semantics):
    return pltpu.CompilerParams(dimension_semantics=semantics,
                                vmem_limit_bytes=VMEM_LIMIT_BYTES)


def _rms_norm_f32(xf, g):
    ms = jnp.mean(xf * xf, axis=-1, keepdims=True)
    return xf * lax.rsqrt(ms + EPS) * g


def _layer_norm_f32(xf, g, b):
    mu = jnp.mean(xf, axis=-1, keepdims=True)
    xc = xf - mu
    var = jnp.mean(xc * xc, axis=-1, keepdims=True)
    return xc * lax.rsqrt(var + EPS) * g + b


def _dot(a, b):
    return jnp.dot(a, b, preferred_element_type=F32)


def _ffn_kernel(x_ref, g_ref, wg_ref, wu_ref, wd_ref, fg_ref, o_ref, h_ref, *,
                f_total, bf, final_norm):
    j = pl.program_id(1)
    nj = pl.num_programs(1)

    @pl.when(j == 0)
    def _():
        xf = x_ref[...]
        h_ref[...] = _rms_norm_f32(xf, g_ref[...]).astype(BF16)
        o_ref[...] = xf

    def step(width):
        h = h_ref[...]
        g = _dot(h, wg_ref[:, :width].astype(BF16))
        u = _dot(h, wu_ref[:, :width].astype(BF16))
        a = (g * jax.nn.sigmoid(g) * u * 0.5).astype(BF16)
        o_ref[...] += _dot(a, wd_ref[:width, :].astype(BF16))

    rem = f_total % bf
    if rem == 0:
        step(bf)
    else:
        pl.when(j < nj - 1)(lambda: step(bf))
        pl.when(j == nj - 1)(lambda: step(rem))

    if final_norm:
        @pl.when(j == nj - 1)
        def _():
            o_ref[...] = _rms_norm_f32(o_ref[...], fg_ref[...])


def _rows3(a):
    return a.reshape(a.shape[0], 1, a.shape[1])


def _row_spec(layer, c):
    return pl.BlockSpec((None, 1, c), lambda *_: (layer, 0, 0))


def _ffn(x2d, layer, norm_g, wg, wu, wd, final_g, *, final_norm, bm=512, bf=256):
    m, d = x2d.shape
    f = wg.shape[2]
    nj = pl.cdiv(f, bf)
    kern = functools.partial(_ffn_kernel, f_total=f, bf=bf, final_norm=final_norm)
    return pl.pallas_call(
        kern,
        grid=(m // bm, nj),
        in_specs=[
            pl.BlockSpec((bm, d), lambda i, j: (i, 0)),
            _row_spec(layer, d),
            pl.BlockSpec((None, d, bf), lambda i, j: (layer, 0, j)),
            pl.BlockSpec((None, d, bf), lambda i, j: (layer, 0, j)),
            pl.BlockSpec((None, bf, d), lambda i, j: (layer, j, 0)),
            pl.BlockSpec((1, d), lambda i, j: (0, 0)),
        ],
        out_specs=pl.BlockSpec((bm, d), lambda i, j: (i, 0)),
        out_shape=jax.ShapeDtypeStruct((m, d), F32),
        scratch_shapes=[pltpu.VMEM((bm, d), BF16)],
        compiler_params=_cparams(("parallel", "arbitrary")),
        name="ffn_final" if final_norm else "ffn",
    )(x2d, _rows3(norm_g), wg, wu, wd, final_g.reshape(1, d))


def _ab_in_kernel(x_ref, g_ref, wa_ref, wgate_ref, wp_ref, a_ref, p_ref, h_ref):
    @pl.when(pl.program_id(1) == 0)
    def _():
        h_ref[...] = _rms_norm_f32(x_ref[...], g_ref[...]).astype(BF16)

    h = h_ref[...]
    a = _dot(h, wa_ref[...].astype(BF16))
    gate = _dot(h, wgate_ref[...].astype(BF16))
    a_ref[...] = a * jax.nn.sigmoid(gate)
    p_ref[...] = _dot(h, wp_ref[...].astype(BF16))


def _ab_in(x2d, layer, norm_g, li, w_in, *, bm=512, bn=256):
    m, d = x2d.shape
    dc = w_in.shape[2] // 3
    nb = dc // bn
    return pl.pallas_call(
        _ab_in_kernel,
        grid=(m // bm, nb),
        in_specs=[
            pl.BlockSpec((bm, d), lambda i, j: (i, 0)),
            _row_spec(layer, d),
            pl.BlockSpec((None, d, bn), lambda i, j: (li, 0, j)),
            pl.BlockSpec((None, d, bn), lambda i, j: (li, 0, nb + j)),
            pl.BlockSpec((None, d, bn), lambda i, j: (li, 0, 2 * nb + j)),
        ],
        out_specs=[pl.BlockSpec((bm, bn), lambda i, j: (i, j)),
                   pl.BlockSpec((bm, bn), lambda i, j: (i, j))],
        out_shape=[jax.ShapeDtypeStruct((m, dc), F32),
                   jax.ShapeDtypeStruct((m, dc), F32)],
        scratch_shapes=[pltpu.VMEM((bm, d), BF16)],
        compiler_params=_cparams(("parallel", "arbitrary")),
        name="ab_in",
    )(x2d, _rows3(norm_g), w_in, w_in, w_in)


def _ab_out_kernel(a_ref, ha_ref, p_ref, hp_ref, cw_ref, cb_ref, lg_ref, lb_ref,
                   pw_ref, ps_ref, x_ref, wo_ref, o_ref,
                   exta_ref, extp_ref, y_ref, cat_ref, *, bm, tiles_per_seq, rb):
    i = pl.program_id(0)
    dc = a_ref.shape[1]
    pg = dc // len(POOL_WINDOWS)

    @pl.when(pl.program_id(1) == 0)
    def _():
        t_in_seq = i % tiles_per_seq
        keep = (t_in_seq > 0).astype(F32)
        exta_ref[0:HALO, :] = ha_ref[...] * keep
        exta_ref[HALO:, :] = a_ref[...]
        extp_ref[0:HALO, :] = hp_ref[...] * keep
        extp_ref[HALO:, :] = p_ref[...]

        base = HALO - (CONV_WIDTH - 1)
        for c in range(dc // LANES):
            cs = slice(c * LANES, (c + 1) * LANES)
            for r in range(bm // rb):
                acc = jnp.broadcast_to(cb_ref[:, cs], (rb, LANES))
                for k in range(CONV_WIDTH):
                    acc = acc + exta_ref[pl.ds(r * rb + base + k, rb), cs] * cw_ref[k:k + 1, cs]
                y_ref[pl.ds(r * rb, rb), cs] = acc
        yn = _layer_norm_f32(y_ref[...], lg_ref[...], lb_ref[...])
        cat_ref[:, 0:dc] = (yn * jax.nn.sigmoid(yn)).astype(BF16)

        pos = t_in_seq * bm + lax.broadcasted_iota(jnp.int32, (bm, 1), 0)
        for gi, w in enumerate(POOL_WINDOWS):
            gs = slice(gi * pg, (gi + 1) * pg)
            tok = extp_ref[pl.ds(HALO, bm), gs]
            s = tok
            for k in range(1, w):
                s = s + extp_ref[pl.ds(HALO - k, bm), gs]
            cnt = jnp.minimum(pos + 1, w).astype(F32)
            pooled = (s / cnt - tok).astype(BF16)
            pb = _dot(pooled, pw_ref[gi].astype(BF16)) * ps_ref[:, gs]
            cat_ref[:, dc + gi * pg: dc + (gi + 1) * pg] = pb.astype(BF16)

    o_ref[...] = x_ref[...] + _dot(cat_ref[...], wo_ref[...].astype(BF16))


def _ab_out(x2d, a, p, li, conv_w, conv_b, ln_g, ln_b, pool_w, pool_scale, w_out, *,
            seq, bm=256, bn=512, rb=64):
    m, d = x2d.shape
    dc = a.shape[1]
    hb = bm // HALO
    kern = functools.partial(_ab_out_kernel, bm=bm, tiles_per_seq=seq // bm, rb=rb)
    halo_map = lambda i, n: (jnp.maximum(i * hb - 1, 0), 0)
    return pl.pallas_call(
        kern,
        grid=(m // bm, d // bn),
        in_specs=[
            pl.BlockSpec((bm, dc), lambda i, n: (i, 0)),
            pl.BlockSpec((HALO, dc), halo_map),
            pl.BlockSpec((bm, dc), lambda i, n: (i, 0)),
            pl.BlockSpec((HALO, dc), halo_map),
            pl.BlockSpec((None,) + conv_w.shape[1:], lambda i, n: (li, 0, 0)),
            _row_spec(li, dc),
            _row_spec(li, dc),
            _row_spec(li, dc),
            pl.BlockSpec((None,) + pool_w.shape[1:], lambda i, n: (li, 0, 0, 0)),
            _row_spec(li, dc),
            pl.BlockSpec((bm, bn), lambda i, n: (i, n)),
            pl.BlockSpec((None, 2 * dc, bn), lambda i, n: (li, 0, n)),
        ],
        out_specs=pl.BlockSpec((bm, bn), lambda i, n: (i, n)),
        out_shape=jax.ShapeDtypeStruct((m, d), F32),
        scratch_shapes=[pltpu.VMEM((bm + HALO, dc), F32),
                        pltpu.VMEM((bm + HALO, dc), F32),
                        pltpu.VMEM((bm, dc), F32),
                        pltpu.VMEM((bm, 2 * dc), BF16)],
        compiler_params=_cparams(("parallel", "arbitrary")),
        name="ab_out",
    )(a, a, p, p, conv_w, _rows3(conv_b), _rows3(ln_g), _rows3(ln_b), pool_w,
      _rows3(pool_scale), x2d, w_out)


def _sgu_in_kernel(x_ref, g_ref, wu_ref, wv_ref, u_ref, v_ref, h_ref):
    @pl.when(pl.program_id(1) == 0)
    def _():
        h_ref[...] = _rms_norm_f32(x_ref[...], g_ref[...]).astype(BF16)

    h = h_ref[...]
    u_ref[...] = jax.nn.gelu(_dot(h, wu_ref[...].astype(BF16)), approximate=True)
    v_ref[...] = jax.nn.gelu(_dot(h, wv_ref[...].astype(BF16)), approximate=True)


def _sgu_in(x2d, layer, norm_g, li, w_in, *, bm=512, bn=256):
    m, d = x2d.shape
    ds = w_in.shape[2] // 2
    nb = ds // bn
    return pl.pallas_call(
        _sgu_in_kernel,
        grid=(m // bm, nb),
        in_specs=[
            pl.BlockSpec((bm, d), lambda i, j: (i, 0)),
            _row_spec(layer, d),
            pl.BlockSpec((None, d, bn), lambda i, j: (li, 0, j)),
            pl.BlockSpec((None, d, bn), lambda i, j: (li, 0, nb + j)),
        ],
        out_specs=[pl.BlockSpec((bm, bn), lambda i, j: (i, j)),
                   pl.BlockSpec((bm, bn), lambda i, j: (i, j))],
        out_shape=[jax.ShapeDtypeStruct((m, ds), F32),
                   jax.ShapeDtypeStruct((m, ds), F32)],
        scratch_shapes=[pltpu.VMEM((bm, d), BF16)],
        compiler_params=_cparams(("parallel", "arbitrary")),
        name="sgu_in",
    )(x2d, _rows3(norm_g), w_in, w_in)


def _sgu_out_kernel(u_ref, v_ref, lg_ref, lb_ref, sw_ref, sbt_ref, x_ref, wo_ref,
                    o_ref, vn_ref, y_ref, *, bm):
    ds = v_ref.shape[1]
    hd = ds // SGU_HEADS

    @pl.when(pl.program_id(1) == 0)
    def _():
        vn_ref[...] = _layer_norm_f32(v_ref[...], lg_ref[...], lb_ref[...]).astype(BF16)
        ri = lax.broadcasted_iota(jnp.int32, (SGU_LEN, SGU_LEN), 0) // CHUNK
        ci = lax.broadcasted_iota(jnp.int32, (SGU_LEN, SGU_LEN), 1) // CHUNK
        mask = ci <= ri
        for g in range(SGU_HEADS):
            wg = jnp.where(mask, sw_ref[g], 0.0).astype(BF16)
            bias = sbt_ref[:, g:g + 1]
            gs = slice(g * hd, (g + 1) * hd)
            for w in range(bm // SGU_LEN):
                rs = pl.ds(w * SGU_LEN, SGU_LEN)
                mixed = _dot(wg, vn_ref[rs, gs]) + bias
                y_ref[rs, gs] = (u_ref[rs, gs] * mixed).astype(BF16)

    o_ref[...] = x_ref[...] + _dot(y_ref[...], wo_ref[...].astype(BF16))


def _sgu_out(x2d, u, v, li, ln_g, ln_b, sgu_w, sgu_b, w_out, *, bm=512, bn=512):
    m, d = x2d.shape
    ds = v.shape[1]
    kern = functools.partial(_sgu_out_kernel, bm=bm)
    return pl.pallas_call(
        kern,
        grid=(m // bm, d // bn),
        in_specs=[
            pl.BlockSpec((bm, ds), lambda i, n: (i, 0)),
            pl.BlockSpec((bm, ds), lambda i, n: (i, 0)),
            _row_spec(li, ds),
            _row_spec(li, ds),
            pl.BlockSpec((None,) + sgu_w.shape[1:], lambda i, n: (li, 0, 0, 0)),
            pl.BlockSpec((None, SGU_LEN, SGU_HEADS), lambda i, n: (li, 0, 0)),
            pl.BlockSpec((bm, bn), lambda i, n: (i, n)),
            pl.BlockSpec((None, ds, bn), lambda i, n: (li, 0, n)),
        ],
        out_specs=pl.BlockSpec((bm, bn), lambda i, n: (i, n)),
        out_shape=jax.ShapeDtypeStruct((m, d), F32),
        scratch_shapes=[pltpu.VMEM((bm, ds), BF16),
                        pltpu.VMEM((bm, ds), BF16)],
        compiler_params=_cparams(("parallel", "arbitrary")),
        name="sgu_out",
    )(u, v, _rows3(ln_g), _rows3(ln_b), sgu_w, jnp.swapaxes(sgu_b, 1, 2), x2d, w_out)


def kernel(x, norm_ffn1, ffn1_w_gate, ffn1_w_up, ffn1_w_down, norm_mix, norm_ffn2,
           ffn2_w_gate, ffn2_w_up, ffn2_w_down, ab_w_in, conv_w, conv_b, conv_ln_g,
           conv_ln_b, pool_w, pool_scale, ab_w_out, c_w_in, sgu_ln_g, sgu_ln_b, sgu_w,
           sgu_b, c_w_out, final_norm):
    bsz, seq, d = x.shape
    depth = norm_ffn1.shape[0]
    x2d = x.reshape(bsz * seq, d)
    for layer in range(depth):
        x2d = _ffn(x2d, layer, norm_ffn1, ffn1_w_gate, ffn1_w_up, ffn1_w_down,
                   final_norm, final_norm=False)
        i = layer // 2
        if layer % 2 == 0:
            a, p = _ab_in(x2d, layer, norm_mix, i, ab_w_in)
            x2d = _ab_out(x2d, a, p, i, conv_w, conv_b, conv_ln_g, conv_ln_b,
                          pool_w, pool_scale, ab_w_out, seq=seq)
        else:
            u, v = _sgu_in(x2d, layer, norm_mix, i, c_w_in)
            x2d = _sgu_out(x2d, u, v, i, sgu_ln_g, sgu_ln_b, sgu_w, sgu_b, c_w_out)
        x2d = _ffn(x2d, layer, norm_ffn2, ffn2_w_gate, ffn2_w_up, ffn2_w_down,
                   final_norm, final_norm=(layer == depth - 1))
    return x2d.reshape(bsz, seq, d)
```

```python
import functools

import jax
import jax.numpy as jnp
from jax import lax
from jax.experimental import pallas as pl
from jax.experimental.pallas import tpu as pltpu

EPS = 1e-6
CHUNK = 64
CONV_WIDTH = 31
POOL_WINDOWS = (2, 4, 8, 16)
SGU_LEN = 128
SGU_HEADS = 8
HALO = 32
LANES = 128
BF16 = jnp.bfloat16
F32 = jnp.float32

VMEM_LIMIT_BYTES = 58 * 1024 * 1024


def _cparams(semantics):
    return pltpu.CompilerParams(dimension_semantics=semantics,
                                vmem_limit_bytes=VMEM_LIMIT_BYTES)


def _rms_norm_f32(xf, g):
    ms = jnp.mean(xf * xf, axis=-1, keepdims=True)
    return xf * lax.rsqrt(ms + EPS) * g


def _layer_norm_f32(xf, g, b):
    mu = jnp.mean(xf, axis=-1, keepdims=True)
    xc = xf - mu
    var = jnp.mean(xc * xc, axis=-1, keepdims=True)
    return xc * lax.rsqrt(var + EPS) * g + b


def _dot(a, b):
    return jnp.dot(a, b, preferred_element_type=F32)


def _ffn_kernel(x_ref, g_ref, wg_ref, wu_ref, wd_ref, fg_ref, o_ref, h_ref, *,
                f_total, bf, final_norm):
    j = pl.program_id(1)
    nj = pl.num_programs(1)

    @pl.when(j == 0)
    def _():
        xf = x_ref[...]
        h_ref[...] = _rms_norm_f32(xf, g_ref[...]).astype(BF16)
        o_ref[...] = xf

    def step(width):
        h = h_ref[...]
        g = _dot(h, wg_ref[:, :width].astype(BF16))
        u = _dot(h, wu_ref[:, :width].astype(BF16))
        a = (g * jax.nn.sigmoid(g) * u * 0.5).astype(BF16)
        o_ref[...] += _dot(a, wd_ref[:width, :].astype(BF16))

    rem = f_total % bf
    if rem == 0:
        step(bf)
    else:
        pl.when(j < nj - 1)(lambda: step(bf))
        pl.when(j == nj - 1)(lambda: step(rem))

    if final_norm:
        @pl.when(j == nj - 1)
        def _():
            o_ref[...] = _rms_norm_f32(o_ref[...], fg_ref[...])


def _rows3(a):
    return a.reshape(a.shape[0], 1, a.shape[1])


def _row_spec(layer, c):
    return pl.BlockSpec((None, 1, c), lambda *_: (layer, 0, 0))


def _ffn(x2d, layer, norm_g, wg, wu, wd, final_g, *, final_norm, bm=1024, bf=256):
    m, d = x2d.shape
    f = wg.shape[2]
    nj = pl.cdiv(f, bf)
    kern = functools.partial(_ffn_kernel, f_total=f, bf=bf, final_norm=final_norm)
    return pl.pallas_call(
        kern,
        grid=(m // bm, nj),
        in_specs=[
            pl.BlockSpec((bm, d), lambda i, j: (i, 0)),
            _row_spec(layer, d),
            pl.BlockSpec((None, d, bf), lambda i, j: (layer, 0, j)),
            pl.BlockSpec((None, d, bf), lambda i, j: (layer, 0, j)),
            pl.BlockSpec((None, bf, d), lambda i, j: (layer, j, 0)),
            pl.BlockSpec((1, d), lambda i, j: (0, 0)),
        ],
        out_specs=pl.BlockSpec((bm, d), lambda i, j: (i, 0)),
        out_shape=jax.ShapeDtypeStruct((m, d), F32),
        scratch_shapes=[pltpu.VMEM((bm, d), BF16)],
        compiler_params=_cparams(("parallel", "arbitrary")),
        name="ffn_final" if final_norm else "ffn",
    )(x2d, _rows3(norm_g), wg, wu, wd, final_g.reshape(1, d))


def _ab_in_kernel(x_ref, g_ref, wa_ref, wgate_ref, wp_ref, a_ref, p_ref, h_ref):
    @pl.when(pl.program_id(1) == 0)
    def _():
        h_ref[...] = _rms_norm_f32(x_ref[...], g_ref[...]).astype(BF16)

    h = h_ref[...]
    a = _dot(h, wa_ref[...].astype(BF16))
    gate = _dot(h, wgate_ref[...].astype(BF16))
    a_ref[...] = a * jax.nn.sigmoid(gate)
    p_ref[...] = _dot(h, wp_ref[...].astype(BF16))


def _ab_in(x2d, layer, norm_g, li, w_in, *, bm=512, bn=256):
    m, d = x2d.shape
    dc = w_in.shape[2] // 3
    nb = dc // bn
    return pl.pallas_call(
        _ab_in_kernel,
        grid=(m // bm, nb),
        in_specs=[
            pl.BlockSpec((bm, d), lambda i, j: (i, 0)),
            _row_spec(layer, d),
            pl.BlockSpec((None, d, bn), lambda i, j: (li, 0, j)),
            pl.BlockSpec((None, d, bn), lambda i, j: (li, 0, nb + j)),
            pl.BlockSpec((None, d, bn), lambda i, j: (li, 0, 2 * nb + j)),
        ],
        out_specs=[pl.BlockSpec((bm, bn), lambda i, j: (i, j)),
                   pl.BlockSpec((bm, bn), lambda i, j: (i, j))],
        out_shape=[jax.ShapeDtypeStruct((m, dc), F32),
                   jax.ShapeDtypeStruct((m, dc), F32)],
        scratch_shapes=[pltpu.VMEM((bm, d), BF16)],
        compiler_params=_cparams(("parallel", "arbitrary")),
        name="ab_in",
    )(x2d, _rows3(norm_g), w_in, w_in, w_in)


def _ab_out_kernel(a_ref, ha_ref, p_ref, hp_ref, cw_ref, cb_ref, lg_ref, lb_ref,
                   pw_ref, ps_ref, x_ref, wo_ref, o_ref,
                   exta_ref, extp_ref, y_ref, cat_ref, *, bm, tiles_per_seq, rb):
    i = pl.program_id(0)
    dc = a_ref.shape[1]
    pg = dc // len(POOL_WINDOWS)

    @pl.when(pl.program_id(1) == 0)
    def _():
        t_in_seq = i % tiles_per_seq
        keep = (t_in_seq > 0).astype(F32)
        exta_ref[0:HALO, :] = ha_ref[...] * keep
        exta_ref[HALO:, :] = a_ref[...]
        extp_ref[0:HALO, :] = hp_ref[...] * keep
        extp_ref[HALO:, :] = p_ref[...]

        base = HALO - (CONV_WIDTH - 1)
        for c in range(dc // LANES):
            cs = slice(c * LANES, (c + 1) * LANES)
            for r in range(bm // rb):
                acc = jnp.broadcast_to(cb_ref[:, cs], (rb, LANES))
                for k in range(CONV_WIDTH):
                    acc = acc + exta_ref[pl.ds(r * rb + base + k, rb), cs] * cw_ref[k:k + 1, cs]
                y_ref[pl.ds(r * rb, rb), cs] = acc
        yn = _layer_norm_f32(y_ref[...], lg_ref[...], lb_ref[...])
        cat_ref[:, 0:dc] = (yn * jax.nn.sigmoid(yn)).astype(BF16)

        pos = t_in_seq * bm + lax.broadcasted_iota(jnp.int32, (bm, 1), 0)
        for gi, w in enumerate(POOL_WINDOWS):
            gs = slice(gi * pg, (gi + 1) * pg)
            tok = extp_ref[pl.ds(HALO, bm), gs]
            s = tok
            for k in range(1, w):
                s = s + extp_ref[pl.ds(HALO - k, bm), gs]
            cnt = jnp.minimum(pos + 1, w).astype(F32)
            pooled = (s / cnt - tok).astype(BF16)
            pb = _dot(pooled, pw_ref[gi].astype(BF16)) * ps_ref[:, gs]
            cat_ref[:, dc + gi * pg: dc + (gi + 1) * pg] = pb.astype(BF16)

    o_ref[...] = x_ref[...] + _dot(cat_ref[...], wo_ref[...].astype(BF16))


def _ab_out(x2d, a, p, li, conv_w, conv_b, ln_g, ln_b, pool_w, pool_scale, w_out, *,
            seq, bm=256, bn=512, rb=64):
    m, d = x2d.shape
    dc = a.shape[1]
    hb = bm // HALO
    kern = functools.partial(_ab_out_kernel, bm=bm, tiles_per_seq=seq // bm, rb=rb)
    halo_map = lambda i, n: (jnp.maximum(i * hb - 1, 0), 0)
    return pl.pallas_call(
        kern,
        grid=(m // bm, d // bn),
        in_specs=[
            pl.BlockSpec((bm, dc), lambda i, n: (i, 0)),
            pl.BlockSpec((HALO, dc), halo_map),
            pl.BlockSpec((bm, dc), lambda i, n: (i, 0)),
            pl.BlockSpec((HALO, dc), halo_map),
            pl.BlockSpec((None,) + conv_w.shape[1:], lambda i, n: (li, 0, 0)),
            _row_spec(li, dc),
            _row_spec(li, dc),
            _row_spec(li, dc),
            pl.BlockSpec((None,) + pool_w.shape[1:], lambda i, n: (li, 0, 0, 0)),
            _row_spec(li, dc),
            pl.BlockSpec((bm, bn), lambda i, n: (i, n)),
            pl.BlockSpec((None, 2 * dc, bn), lambda i, n: (li, 0, n)),
        ],
        out_specs=pl.BlockSpec((bm, bn), lambda i, n: (i, n)),
        out_shape=jax.ShapeDtypeStruct((m, d), F32),
        scratch_shapes=[pltpu.VMEM((bm + HALO, dc), F32),
                        pltpu.VMEM((bm + HALO, dc), F32),
                        pltpu.VMEM((bm, dc), F32),
                        pltpu.VMEM((bm, 2 * dc), BF16)],
        compiler_params=_cparams(("parallel", "arbitrary")),
        name="ab_out",
    )(a, a, p, p, conv_w, _rows3(conv_b), _rows3(ln_g), _rows3(ln_b), pool_w,
      _rows3(pool_scale), x2d, w_out)


def _sgu_in_kernel(x_ref, g_ref, wu_ref, wv_ref, u_ref, v_ref, h_ref):
    @pl.when(pl.program_id(1) == 0)
    def _():
        h_ref[...] = _rms_norm_f32(x_ref[...], g_ref[...]).astype(BF16)

    h = h_ref[...]
    u_ref[...] = jax.nn.gelu(_dot(h, wu_ref[...].astype(BF16)), approximate=True)
    v_ref[...] = jax.nn.gelu(_dot(h, wv_ref[...].astype(BF16)), approximate=True)


def _sgu_in(x2d, layer, norm_g, li, w_in, *, bm=512, bn=256):
    m, d = x2d.shape
    ds = w_in.shape[2] // 2
    nb = ds // bn
    return pl.pallas_call(
        _sgu_in_kernel,
        grid=(m // bm, nb),
        in_specs=[
            pl.BlockSpec((bm, d), lambda i, j: (i, 0)),
            _row_spec(layer, d),
            pl.BlockSpec((None, d, bn), lambda i, j: (li, 0, j)),
            pl.BlockSpec((None, d, bn), lambda i, j: (li, 0, nb + j)),
        ],
        out_specs=[pl.BlockSpec((bm, bn), lambda i, j: (i, j)),
                   pl.BlockSpec((bm, bn), lambda i, j: (i, j))],
        out_shape=[jax.ShapeDtypeStruct((m, ds), F32),
                   jax.ShapeDtypeStruct((m, ds), F32)],
        scratch_shapes=[pltpu.VMEM((bm, d), BF16)],
        compiler_params=_cparams(("parallel", "arbitrary")),
        name="sgu_in",
    )(x2d, _rows3(norm_g), w_in, w_in)


def _sgu_out_kernel(u_ref, v_ref, lg_ref, lb_ref, sw_ref, sbt_ref, x_ref, wo_ref,
                    o_ref, vn_ref, y_ref, *, bm):
    ds = v_ref.shape[1]
    hd = ds // SGU_HEADS

    @pl.when(pl.program_id(1) == 0)
    def _():
        vn_ref[...] = _layer_norm_f32(v_ref[...], lg_ref[...], lb_ref[...]).astype(BF16)
        ri = lax.broadcasted_iota(jnp.int32, (SGU_LEN, SGU_LEN), 0) // CHUNK
        ci = lax.broadcasted_iota(jnp.int32, (SGU_LEN, SGU_LEN), 1) // CHUNK
        mask = ci <= ri
        for g in range(SGU_HEADS):
            wg = jnp.where(mask, sw_ref[g], 0.0).astype(BF16)
            bias = sbt_ref[:, g:g + 1]
            gs = slice(g * hd, (g + 1) * hd)
            for w in range(bm // SGU_LEN):
                rs = pl.ds(w * SGU_LEN, SGU_LEN)
                mixed = _dot(wg, vn_ref[rs, gs]) + bias
                y_ref[rs, gs] = (u_ref[rs, gs] * mixed).astype(BF16)

    o_ref[...] = x_ref[...] + _dot(y_ref[...], wo_ref[...].astype(BF16))


def _sgu_out(x2d, u, v, li, ln_g, ln_b, sgu_w, sgu_b, w_out, *, bm=512, bn=512):
    m, d = x2d.shape
    ds = v.shape[1]
    kern = functools.partial(_sgu_out_kernel, bm=bm)
    return pl.pallas_call(
        kern,
        grid=(m // bm, d // bn),
        in_specs=[
            pl.BlockSpec((bm, ds), lambda i, n: (i, 0)),
            pl.BlockSpec((bm, ds), lambda i, n: (i, 0)),
            _row_spec(li, ds),
            _row_spec(li, ds),
            pl.BlockSpec((None,) + sgu_w.shape[1:], lambda i, n: (li, 0, 0, 0)),
            pl.BlockSpec((None, SGU_LEN, SGU_HEADS), lambda i, n: (li, 0, 0)),
            pl.BlockSpec((bm, bn), lambda i, n: (i, n)),
            pl.BlockSpec((None, ds, bn), lambda i, n: (li, 0, n)),
        ],
        out_specs=pl.BlockSpec((bm, bn), lambda i, n: (i, n)),
        out_shape=jax.ShapeDtypeStruct((m, d), F32),
        scratch_shapes=[pltpu.VMEM((bm, ds), BF16),
                        pltpu.VMEM((bm, ds), BF16)],
        compiler_params=_cparams(("parallel", "arbitrary")),
        name="sgu_out",
    )(u, v, _rows3(ln_g), _rows3(ln_b), sgu_w, jnp.swapaxes(sgu_b, 1, 2), x2d, w_out)


def kernel(x, norm_ffn1, ffn1_w_gate, ffn1_w_up, ffn1_w_down, norm_mix, norm_ffn2,
           ffn2_w_gate, ffn2_w_up, ffn2_w_down, ab_w_in, conv_w, conv_b, conv_ln_g,
           conv_ln_b, pool_w, pool_scale, ab_w_out, c_w_in, sgu_ln_g, sgu_ln_b, sgu_w,
           sgu_b, c_w_out, final_norm):
    bsz, seq, d = x.shape
    depth = norm_ffn1.shape[0]
    x2d = x.reshape(bsz * seq, d)
    for layer in range(depth):
        x2d = _ffn(x2d, layer, norm_ffn1, ffn1_w_gate, ffn1_w_up, ffn1_w_down,
                   final_norm, final_norm=False)
        i = layer // 2
        if layer % 2 == 0:
            a, p = _ab_in(x2d, layer, norm_mix, i, ab_w_in)
            x2d = _ab_out(x2d, a, p, i, conv_w, conv_b, conv_ln_g, conv_ln_b,
                          pool_w, pool_scale, ab_w_out, seq=seq)
        else:
            u, v = _sgu_in(x2d, layer, norm_mix, i, c_w_in)
            x2d = _sgu_out(x2d, u, v, i, sgu_ln_g, sgu_ln_b, sgu_w, sgu_b, c_w_out)
        x2d = _ffn(x2d, layer, norm_ffn2, ffn2_w_gate, ffn2_w_up, ffn2_w_down,
                   final_norm, final_norm=(layer == depth - 1))
    return x2d.reshape(bsz, seq, d)
```

```python
import functools

import jax
import jax.numpy as jnp
from jax import lax
from jax.experimental import pallas as pl
from jax.experimental.pallas import tpu as pltpu

EPS = 1e-6
CHUNK = 64
CONV_WIDTH = 31
POOL_WINDOWS = (2, 4, 8, 16)
SGU_LEN = 128
SGU_HEADS = 8
HALO = 32
SUBLANES = 8
LANES = 128
ROW_BLOCK = 64
BF16 = jnp.bfloat16
F32 = jnp.float32

VMEM_LIMIT_BYTES = 58 * 1024 * 1024


def _cparams(semantics):
    return pltpu.CompilerParams(dimension_semantics=semantics,
                                vmem_limit_bytes=VMEM_LIMIT_BYTES)


def _rms_norm_f32(xf, g):
    ms = jnp.mean(xf * xf, axis=-1, keepdims=True)
    return xf * lax.rsqrt(ms + EPS) * g


def _layer_norm_f32(xf, g, b):
    mu = jnp.mean(xf, axis=-1, keepdims=True)
    xc = xf - mu
    var = jnp.mean(xc * xc, axis=-1, keepdims=True)
    return xc * lax.rsqrt(var + EPS) * g + b


def _dot(a, b):
    return jnp.dot(a, b, preferred_element_type=F32)


def _rows3(a):
    return a.reshape(a.shape[0], 1, a.shape[1])


def _row_spec(layer, c):
    return pl.BlockSpec((None, 1, c), lambda *_: (layer, 0, 0))


def _ffn_kernel(x_hbm, g_ref, wg_ref, wu_ref, wd_ref, fg_ref, o_ref, h_ref, xbuf, sem, *,
                bm, f_total, bf, final_norm):
    i = pl.program_id(0)
    j = pl.program_id(1)
    ni = pl.num_programs(0)
    nj = pl.num_programs(1)

    def x_copy(tile):
        return pltpu.make_async_copy(x_hbm.at[pl.ds(tile * bm, bm), :], xbuf, sem)

    @pl.when(j == 0)
    def _():
        @pl.when(i == 0)
        def _():
            x_copy(0).start()

        x_copy(i).wait()
        xf = xbuf[...]
        h_ref[...] = _rms_norm_f32(xf, g_ref[...]).astype(BF16)
        o_ref[...] = xf

    @pl.when((j == 1) & (i + 1 < ni))
    def _():
        x_copy(i + 1).start()

    def step(width):
        h = h_ref[...]
        g = _dot(h, wg_ref[:, :width].astype(BF16))
        u = _dot(h, wu_ref[:, :width].astype(BF16))
        a = (g * jax.nn.sigmoid(g) * u * 0.5).astype(BF16)
        o_ref[...] += _dot(a, wd_ref[:width, :].astype(BF16))

    rem = f_total % bf
    if rem == 0:
        step(bf)
    else:
        pl.when(j < nj - 1)(lambda: step(bf))
        pl.when(j == nj - 1)(lambda: step(rem))

    if final_norm:
        @pl.when(j == nj - 1)
        def _():
            o_ref[...] = _rms_norm_f32(o_ref[...], fg_ref[...])


def _ffn(x2d, layer, norm_g, wg, wu, wd, final_g, *, final_norm, bm=1024, bf=512):
    m, d = x2d.shape
    f = wg.shape[2]
    nj = pl.cdiv(f, bf)
    assert nj >= 2 and m % bm == 0
    kern = functools.partial(_ffn_kernel, bm=bm, f_total=f, bf=bf, final_norm=final_norm)
    return pl.pallas_call(
        kern,
        grid=(m // bm, nj),
        in_specs=[
            pl.BlockSpec(memory_space=pl.ANY),
            _row_spec(layer, d),
            pl.BlockSpec((None, d, bf), lambda i, j: (layer, 0, j)),
            pl.BlockSpec((None, d, bf), lambda i, j: (layer, 0, j)),
            pl.BlockSpec((None, bf, d), lambda i, j: (layer, j, 0)),
            pl.BlockSpec((1, d), lambda i, j: (0, 0)),
        ],
        out_specs=pl.BlockSpec((bm, d), lambda i, j: (i, 0)),
        out_shape=jax.ShapeDtypeStruct((m, d), F32),
        scratch_shapes=[pltpu.VMEM((bm, d), BF16),
                        pltpu.VMEM((bm, d), F32),
                        pltpu.SemaphoreType.DMA(())],
        compiler_params=_cparams(("arbitrary", "arbitrary")),
        name="ffn_final" if final_norm else "ffn",
    )(x2d, _rows3(norm_g), wg, wu, wd, final_g.reshape(1, d))


def _ab_in_kernel(x_ref, g_ref, wa_ref, wgate_ref, wp_ref, a_ref, p_ref, h_ref):
    @pl.when(pl.program_id(1) == 0)
    def _():
        h_ref[...] = _rms_norm_f32(x_ref[...], g_ref[...]).astype(BF16)

    h = h_ref[...]
    a = _dot(h, wa_ref[...].astype(BF16))
    gate = _dot(h, wgate_ref[...].astype(BF16))
    a_ref[...] = a * jax.nn.sigmoid(gate)
    p_ref[...] = _dot(h, wp_ref[...].astype(BF16))


def _ab_in(x2d, layer, norm_g, li, w_in, *, bm=1024, bn=256):
    m, d = x2d.shape
    dc = w_in.shape[2] // 3
    nb = dc // bn
    return pl.pallas_call(
        _ab_in_kernel,
        grid=(m // bm, nb),
        in_specs=[
            pl.BlockSpec((bm, d), lambda i, j: (i, 0)),
            _row_spec(layer, d),
            pl.BlockSpec((None, d, bn), lambda i, j: (li, 0, j)),
            pl.BlockSpec((None, d, bn), lambda i, j: (li, 0, nb + j)),
            pl.BlockSpec((None, d, bn), lambda i, j: (li, 0, 2 * nb + j)),
        ],
        out_specs=[pl.BlockSpec((bm, bn), lambda i, j: (i, j)),
                   pl.BlockSpec((bm, bn), lambda i, j: (i, j))],
        out_shape=[jax.ShapeDtypeStruct((m, dc), F32),
                   jax.ShapeDtypeStruct((m, dc), F32)],
        scratch_shapes=[pltpu.VMEM((bm, d), BF16)],
        compiler_params=_cparams(("parallel", "arbitrary")),
        name="ab_in",
    )(x2d, _rows3(norm_g), w_in, w_in, w_in)


def _conv_block(ext_ref, cw_ref, cb_ref, r0, cs):
    rb = ROW_BLOCK
    base = HALO - (CONV_WIDTH - 1)
    e = ext_ref[pl.ds(r0, rb + HALO), cs]
    acc = jnp.broadcast_to(cb_ref[:, cs], (rb, LANES))
    for r in range(SUBLANES):
        qs = [q for q in range((base + CONV_WIDTH - 1) // SUBLANES + 1)
              if 0 <= SUBLANES * q + r - base < CONV_WIDTH]
        s = e if r == 0 else pltpu.roll(e, rb + HALO - r, 0)
        for q in qs:
            k = SUBLANES * q + r - base
            acc = acc + s[SUBLANES * q:SUBLANES * q + rb] * cw_ref[k:k + 1, cs]
    return acc


def _pool_block(ext_ref, r0, cs, w, pos):
    rb = ROW_BLOCK
    x = ext_ref[pl.ds(r0, rb + HALO), cs]
    tok = x[HALO:HALO + rb]
    s = x
    start, width = 0, 1
    while width < w:
        new_start = start + SUBLANES
        n = rb + HALO - new_start
        off = new_start - start
        s = s[off:off + n] + s[off - width:off - width + n]
        start, width = new_start, 2 * width
    s = s[HALO - start:HALO - start + rb]
    cnt = jnp.minimum(pos + 1, w).astype(F32)
    return s / cnt - tok


def _ab_out_kernel(a_ref, ha_ref, p_ref, hp_ref, cw_ref, cb_ref, lg_ref, lb_ref,
                   pw_ref, ps_ref, x_ref, wo_ref, o_ref,
                   exta_ref, extp_ref, y_ref, pooled_ref, cat_ref, *, bm, tiles_per_seq):
    i = pl.program_id(0)
    dc = a_ref.shape[1]
    pg = dc // len(POOL_WINDOWS)
    rb = ROW_BLOCK

    @pl.when(pl.program_id(1) == 0)
    def _():
        t_in_seq = i % tiles_per_seq
        keep = (t_in_seq > 0).astype(F32)
        exta_ref[0:HALO, :] = ha_ref[...] * keep
        exta_ref[HALO:, :] = a_ref[...]
        extp_ref[0:HALO, :] = hp_ref[...] * keep
        extp_ref[HALO:, :] = p_ref[...]

        for c in range(dc // LANES):
            cs = slice(c * LANES, (c + 1) * LANES)
            w = POOL_WINDOWS[c * LANES // pg]
            for r in range(bm // rb):
                y_ref[pl.ds(r * rb, rb), cs] = _conv_block(exta_ref, cw_ref, cb_ref, r * rb, cs)
                pos = t_in_seq * bm + r * rb + lax.broadcasted_iota(jnp.int32, (rb, 1), 0)
                pooled_ref[pl.ds(r * rb, rb), cs] = _pool_block(
                    extp_ref, r * rb, cs, w, pos).astype(BF16)

        yn = _layer_norm_f32(y_ref[...], lg_ref[...], lb_ref[...])
        cat_ref[:, 0:dc] = (yn * jax.nn.sigmoid(yn)).astype(BF16)
        for gi in range(len(POOL_WINDOWS)):
            gs = slice(gi * pg, (gi + 1) * pg)
            pb = _dot(pooled_ref[:, gs], pw_ref[gi].astype(BF16)) * ps_ref[:, gs]
            cat_ref[:, dc + gi * pg: dc + (gi + 1) * pg] = pb.astype(BF16)

    o_ref[...] = x_ref[...] + _dot(cat_ref[...], wo_ref[...].astype(BF16))


def _ab_out(x2d, a, p, li, conv_w, conv_b, ln_g, ln_b, pool_w, pool_scale, w_out, *,
            seq, bm=512, bn=512):
    m, d = x2d.shape
    dc = a.shape[1]
    hb = bm // HALO
    assert seq % bm == 0 and bm % ROW_BLOCK == 0 and HALO >= CONV_WIDTH - 1
    assert HALO >= max(POOL_WINDOWS) and (dc // len(POOL_WINDOWS)) % LANES == 0
    kern = functools.partial(_ab_out_kernel, bm=bm, tiles_per_seq=seq // bm)
    halo_map = lambda i, n: (jnp.maximum(i * hb - 1, 0), 0)
    return pl.pallas_call(
        kern,
        grid=(m // bm, d // bn),
        in_specs=[
            pl.BlockSpec((bm, dc), lambda i, n: (i, 0)),
            pl.BlockSpec((HALO, dc), halo_map),
            pl.BlockSpec((bm, dc), lambda i, n: (i, 0)),
            pl.BlockSpec((HALO, dc), halo_map),
            pl.BlockSpec((None,) + conv_w.shape[1:], lambda i, n: (li, 0, 0)),
            _row_spec(li, dc),
            _row_spec(li, dc),
            _row_spec(li, dc),
            pl.BlockSpec((None,) + pool_w.shape[1:], lambda i, n: (li, 0, 0, 0)),
            _row_spec(li, dc),
            pl.BlockSpec((bm, bn), lambda i, n: (i, n)),
            pl.BlockSpec((None, 2 * dc, bn), lambda i, n: (li, 0, n)),
        ],
        out_specs=pl.BlockSpec((bm, bn), lambda i, n: (i, n)),
        out_shape=jax.ShapeDtypeStruct((m, d), F32),
        scratch_shapes=[pltpu.VMEM((bm + HALO, dc), F32),
                        pltpu.VMEM((bm + HALO, dc), F32),
                        pltpu.VMEM((bm, dc), F32),
                        pltpu.VMEM((bm, dc), BF16),
                        pltpu.VMEM((bm, 2 * dc), BF16)],
        compiler_params=_cparams(("parallel", "arbitrary")),
        name="ab_out",
    )(a, a, p, p, conv_w, _rows3(conv_b), _rows3(ln_g), _rows3(ln_b), pool_w,
      _rows3(pool_scale), x2d, w_out)


def _sgu_in_kernel(x_ref, g_ref, wu_ref, wv_ref, lg_ref, lb_ref, u_ref, vn_ref,
                   h_ref, v_ref):
    j = pl.program_id(1)
    nb, _, bn = v_ref.shape

    @pl.when(j == 0)
    def _():
        h_ref[...] = _rms_norm_f32(x_ref[...], g_ref[...]).astype(BF16)

    h = h_ref[...]
    u_ref[...] = jax.nn.gelu(_dot(h, wu_ref[...].astype(BF16)), approximate=True).astype(BF16)
    v_ref[j] = jax.nn.gelu(_dot(h, wv_ref[...].astype(BF16)), approximate=True)

    @pl.when(j == nb - 1)
    def _():
        width = nb * bn
        total = v_ref[0].sum(axis=-1, keepdims=True)
        for c in range(1, nb):
            total = total + v_ref[c].sum(axis=-1, keepdims=True)
        mu = total / width
        sq = None
        for c in range(nb):
            xc = v_ref[c] - mu
            part = (xc * xc).sum(axis=-1, keepdims=True)
            sq = part if sq is None else sq + part
        scale = lax.rsqrt(sq / width + EPS)
        for c in range(nb):
            cs = slice(c * bn, (c + 1) * bn)
            vn = (v_ref[c] - mu) * scale * lg_ref[:, cs] + lb_ref[:, cs]
            vn_ref[:, cs] = vn.astype(BF16)


def _sgu_in(x2d, layer, norm_g, li, w_in, ln_g, ln_b, *, bm=1024, bn=256):
    m, d = x2d.shape
    ds = w_in.shape[2] // 2
    nb = ds // bn
    return pl.pallas_call(
        _sgu_in_kernel,
        grid=(m // bm, nb),
        in_specs=[
            pl.BlockSpec((bm, d), lambda i, j: (i, 0)),
            _row_spec(layer, d),
            pl.BlockSpec((None, d, bn), lambda i, j: (li, 0, j)),
            pl.BlockSpec((None, d, bn), lambda i, j: (li, 0, nb + j)),
            _row_spec(li, ds),
            _row_spec(li, ds),
        ],
        out_specs=[pl.BlockSpec((bm, bn), lambda i, j: (i, j)),
                   pl.BlockSpec((bm, ds), lambda i, j: (i, 0))],
        out_shape=[jax.ShapeDtypeStruct((m, ds), BF16),
                   jax.ShapeDtypeStruct((m, ds), BF16)],
        scratch_shapes=[pltpu.VMEM((bm, d), BF16),
                        pltpu.VMEM((nb, bm, bn), F32)],
        compiler_params=_cparams(("parallel", "arbitrary")),
        name="sgu_in",
    )(x2d, _rows3(norm_g), w_in, w_in, _rows3(ln_g), _rows3(ln_b))


def _sgu_out_kernel(u_ref, vn_ref, sw_ref, sbt_ref, x_ref, wo_ref, o_ref, y_ref, *, bm):
    ds = vn_ref.shape[1]
    hd = ds // SGU_HEADS

    @pl.when(pl.program_id(1) == 0)
    def _():
        ri = lax.broadcasted_iota(jnp.int32, (SGU_LEN, SGU_LEN), 0) // CHUNK
        ci = lax.broadcasted_iota(jnp.int32, (SGU_LEN, SGU_LEN), 1) // CHUNK
        mask = ci <= ri
        for g in range(SGU_HEADS):
            wg = jnp.where(mask, sw_ref[g], 0.0).astype(BF16)
            bias = sbt_ref[:, g:g + 1]
            gs = slice(g * hd, (g + 1) * hd)
            for w in range(bm // SGU_LEN):
                rs = pl.ds(w * SGU_LEN, SGU_LEN)
                mixed = _dot(wg, vn_ref[rs, gs]) + bias
                y_ref[rs, gs] = (u_ref[rs, gs].astype(F32) * mixed).astype(BF16)

    o_ref[...] = x_ref[...] + _dot(y_ref[...], wo_ref[...].astype(BF16))


def _sgu_out(x2d, u, vn, li, sgu_w, sgu_b, w_out, *, bm=1024, bn=512):
    m, d = x2d.shape
    ds = vn.shape[1]
    kern = functools.partial(_sgu_out_kernel, bm=bm)
    return pl.pallas_call(
        kern,
        grid=(m // bm, d // bn),
        in_specs=[
            pl.BlockSpec((bm, ds), lambda i, n: (i, 0)),
            pl.BlockSpec((bm, ds), lambda i, n: (i, 0)),
            pl.BlockSpec((None,) + sgu_w.shape[1:], lambda i, n: (li, 0, 0, 0)),
            pl.BlockSpec((None, SGU_LEN, SGU_HEADS), lambda i, n: (li, 0, 0)),
            pl.BlockSpec((bm, bn), lambda i, n: (i, n)),
            pl.BlockSpec((None, ds, bn), lambda i, n: (li, 0, n)),
        ],
        out_specs=pl.BlockSpec((bm, bn), lambda i, n: (i, n)),
        out_shape=jax.ShapeDtypeStruct((m, d), F32),
        scratch_shapes=[pltpu.VMEM((bm, ds), BF16)],
        compiler_params=_cparams(("parallel", "arbitrary")),
        name="sgu_out",
    )(u, vn, sgu_w, jnp.swapaxes(sgu_b, 1, 2), x2d, w_out)


def kernel(x, norm_ffn1, ffn1_w_gate, ffn1_w_up, ffn1_w_down, norm_mix, norm_ffn2,
           ffn2_w_gate, ffn2_w_up, ffn2_w_down, ab_w_in, conv_w, conv_b, conv_ln_g,
           conv_ln_b, pool_w, pool_scale, ab_w_out, c_w_in, sgu_ln_g, sgu_ln_b, sgu_w,
           sgu_b, c_w_out, final_norm):
    bsz, seq, d = x.shape
    depth = norm_ffn1.shape[0]
    x2d = x.reshape(bsz * seq, d)
    for layer in range(depth):
        x2d = _ffn(x2d, layer, norm_ffn1, ffn1_w_gate, ffn1_w_up, ffn1_w_down,
                   final_norm, final_norm=False)
        i = layer // 2
        if layer % 2 == 0:
            a, p = _ab_in(x2d, layer, norm_mix, i, ab_w_in)
            x2d = _ab_out(x2d, a, p, i, conv_w, conv_b, conv_ln_g, conv_ln_b,
                          pool_w, pool_scale, ab_w_out, seq=seq)
        else:
            u, vn = _sgu_in(x2d, layer, norm_mix, i, c_w_in, sgu_ln_g, sgu_ln_b)
            x2d = _sgu_out(x2d, u, vn, i, sgu_w, sgu_b, c_w_out)
        x2d = _ffn(x2d, layer, norm_ffn2, ffn2_w_gate, ffn2_w_up, ffn2_w_down,
                   final_norm, final_norm=(layer == depth - 1))
    return x2d.reshape(bsz, seq, d)
```

```python
import functools

import jax
import jax.numpy as jnp
from jax import lax
from jax.experimental import pallas as pl
from jax.experimental.pallas import tpu as pltpu

EPS = 1e-6
CHUNK = 64
CONV_WIDTH = 31
POOL_WINDOWS = (2, 4, 8, 16)
SGU_LEN = 128
SGU_HEADS = 8
HALO = 32
SUBLANES = 8
LANES = 128
ROW_BLOCK = 64
BF16 = jnp.bfloat16
F32 = jnp.float32

VMEM_LIMIT_BYTES = 58 * 1024 * 1024


def _cparams(semantics):
    return pltpu.CompilerParams(dimension_semantics=semantics,
                                vmem_limit_bytes=VMEM_LIMIT_BYTES)


def _rms_norm_f32(xf, g):
    ms = jnp.mean(xf * xf, axis=-1, keepdims=True)
    return xf * lax.rsqrt(ms + EPS) * g


def _layer_norm_f32(xf, g, b):
    mu = jnp.mean(xf, axis=-1, keepdims=True)
    xc = xf - mu
    var = jnp.mean(xc * xc, axis=-1, keepdims=True)
    return xc * lax.rsqrt(var + EPS) * g + b


def _dot(a, b):
    return jnp.dot(a, b, preferred_element_type=F32)


def _rows3(a):
    return a.reshape(a.shape[0], 1, a.shape[1])


def _row_spec(layer, c):
    return pl.BlockSpec((None, 1, c), lambda *_: (layer, 0, 0))


def _ffn_kernel(x_hbm, g_ref, wg_ref, wu_ref, wd_ref, fg_ref, o_ref, h_ref, xbuf, sem, *,
                bm, f_total, bf, final_norm):
    i = pl.program_id(0)
    j = pl.program_id(1)
    ni = pl.num_programs(0)
    nj = pl.num_programs(1)

    def x_copy(tile):
        return pltpu.make_async_copy(x_hbm.at[pl.ds(tile * bm, bm), :], xbuf, sem)

    @pl.when((i == 0) & (j == 0))
    def _():
        x_copy(0).start()

    @pl.when((j == 1) & (i + 1 < ni))
    def _():
        x_copy(i + 1).start()

    def step(width, first=False, last=False):
        if first:
            x_copy(i).wait()
            h_ref[...] = _rms_norm_f32(xbuf[...], g_ref[...]).astype(BF16)
        h = h_ref[...]
        g = _dot(h, wg_ref[:, :width].astype(BF16))
        u = _dot(h, wu_ref[:, :width].astype(BF16))
        a = (g * jax.nn.sigmoid(g) * u * 0.5).astype(BF16)
        down = _dot(a, wd_ref[:width, :].astype(BF16))
        if first:
            o_ref[...] = xbuf[...] + down
        else:
            o_ref[...] += down
        if last and final_norm:
            o_ref[...] = _rms_norm_f32(o_ref[...], fg_ref[...])

    rem = f_total % bf
    pl.when(j == 0)(lambda: step(bf, first=True))
    pl.when((j > 0) & (j < nj - 1))(lambda: step(bf))
    pl.when(j == nj - 1)(lambda: step(rem if rem else bf, last=True))


def _ffn(x2d, layer, norm_g, wg, wu, wd, final_g, *, final_norm, bm=1024, bf=512):
    m, d = x2d.shape
    f = wg.shape[2]
    nj = pl.cdiv(f, bf)
    assert nj >= 3 and m % bm == 0
    kern = functools.partial(_ffn_kernel, bm=bm, f_total=f, bf=bf, final_norm=final_norm)
    return pl.pallas_call(
        kern,
        grid=(m // bm, nj),
        in_specs=[
            pl.BlockSpec(memory_space=pl.ANY),
            _row_spec(layer, d),
            pl.BlockSpec((None, d, bf), lambda i, j: (layer, 0, j)),
            pl.BlockSpec((None, d, bf), lambda i, j: (layer, 0, j)),
            pl.BlockSpec((None, bf, d), lambda i, j: (layer, j, 0)),
            pl.BlockSpec((1, d), lambda i, j: (0, 0)),
        ],
        out_specs=pl.BlockSpec((bm, d), lambda i, j: (i, 0)),
        out_shape=jax.ShapeDtypeStruct((m, d), F32),
        scratch_shapes=[pltpu.VMEM((bm, d), BF16),
                        pltpu.VMEM((bm, d), F32),
                        pltpu.SemaphoreType.DMA(())],
        compiler_params=_cparams(("arbitrary", "arbitrary")),
        name="ffn_final" if final_norm else "ffn",
    )(x2d, _rows3(norm_g), wg, wu, wd, final_g.reshape(1, d))


def _cached_col_map(layer, first_block, nb):
    return lambda i, j: (layer, 0, first_block + jnp.where(i == 0, j, nb - 1))


def _ab_in_kernel(x_ref, g_ref, wa_ref, wgate_ref, wp_ref, a_ref, p_ref, h_ref, wc_ref):
    i = pl.program_id(0)
    j = pl.program_id(1)

    @pl.when(j == 0)
    def _():
        h_ref[...] = _rms_norm_f32(x_ref[...], g_ref[...]).astype(BF16)

    @pl.when(i == 0)
    def _():
        wc_ref[0, j] = wa_ref[...].astype(BF16)
        wc_ref[1, j] = wgate_ref[...].astype(BF16)
        wc_ref[2, j] = wp_ref[...].astype(BF16)

    h = h_ref[...]
    a = _dot(h, wc_ref[0, j])
    gate = _dot(h, wc_ref[1, j])
    a_ref[...] = a * jax.nn.sigmoid(gate)
    p_ref[...] = _dot(h, wc_ref[2, j])


def _ab_in(x2d, layer, norm_g, li, w_in, *, bm=1024, bn=256):
    m, d = x2d.shape
    dc = w_in.shape[2] // 3
    nb = dc // bn
    return pl.pallas_call(
        _ab_in_kernel,
        grid=(m // bm, nb),
        in_specs=[
            pl.BlockSpec((bm, d), lambda i, j: (i, 0)),
            _row_spec(layer, d),
            pl.BlockSpec((None, d, bn), _cached_col_map(li, 0, nb)),
            pl.BlockSpec((None, d, bn), _cached_col_map(li, nb, nb)),
            pl.BlockSpec((None, d, bn), _cached_col_map(li, 2 * nb, nb)),
        ],
        out_specs=[pl.BlockSpec((bm, bn), lambda i, j: (i, j)),
                   pl.BlockSpec((bm, bn), lambda i, j: (i, j))],
        out_shape=[jax.ShapeDtypeStruct((m, dc), F32),
                   jax.ShapeDtypeStruct((m, dc), F32)],
        scratch_shapes=[pltpu.VMEM((bm, d), BF16),
                        pltpu.VMEM((3, nb, d, bn), BF16)],
        compiler_params=_cparams(("arbitrary", "arbitrary")),
        name="ab_in",
    )(x2d, _rows3(norm_g), w_in, w_in, w_in)


def _conv_block(ext_ref, cw_ref, cb_ref, r0, cs):
    rb = ROW_BLOCK
    base = HALO - (CONV_WIDTH - 1)
    e = ext_ref[pl.ds(r0, rb + HALO), cs]
    acc = jnp.broadcast_to(cb_ref[:, cs], (rb, LANES))
    for r in range(SUBLANES):
        qs = [q for q in range((base + CONV_WIDTH - 1) // SUBLANES + 1)
              if 0 <= SUBLANES * q + r - base < CONV_WIDTH]
        s = e if r == 0 else pltpu.roll(e, rb + HALO - r, 0)
        for q in qs:
            k = SUBLANES * q + r - base
            acc = acc + s[SUBLANES * q:SUBLANES * q + rb] * cw_ref[k:k + 1, cs]
    return acc


def _pool_block(ext_ref, r0, cs, w, pos):
    rb = ROW_BLOCK
    x = ext_ref[pl.ds(r0, rb + HALO), cs]
    tok = x[HALO:HALO + rb]
    s = x
    start, width = 0, 1
    while width < w:
        new_start = start + SUBLANES
        n = rb + HALO - new_start
        off = new_start - start
        s = s[off:off + n] + s[off - width:off - width + n]
        start, width = new_start, 2 * width
    s = s[HALO - start:HALO - start + rb]
    cnt = jnp.minimum(pos + 1, w).astype(F32)
    return s / cnt - tok


def _ab_out_kernel(a_ref, ha_ref, p_ref, hp_ref, cw_ref, cb_ref, lg_ref, lb_ref,
                   pw_ref, ps_ref, x_ref, wo_ref, o_ref,
                   exta_ref, extp_ref, y_ref, pooled_ref, cat_ref, woc_ref, *, bm,
                   tiles_per_seq):
    i = pl.program_id(0)
    dc = a_ref.shape[1]
    pg = dc // len(POOL_WINDOWS)
    rb = ROW_BLOCK

    @pl.when(pl.program_id(1) == 0)
    def _():
        t_in_seq = i % tiles_per_seq
        keep = (t_in_seq > 0).astype(F32)
        exta_ref[0:HALO, :] = ha_ref[...] * keep
        exta_ref[HALO:, :] = a_ref[...]
        extp_ref[0:HALO, :] = hp_ref[...] * keep
        extp_ref[HALO:, :] = p_ref[...]

        for c in range(dc // LANES):
            cs = slice(c * LANES, (c + 1) * LANES)
            w = POOL_WINDOWS[c * LANES // pg]
            for r in range(bm // rb):
                y_ref[pl.ds(r * rb, rb), cs] = _conv_block(exta_ref, cw_ref, cb_ref, r * rb, cs)
                pos = t_in_seq * bm + r * rb + lax.broadcasted_iota(jnp.int32, (rb, 1), 0)
                pooled_ref[pl.ds(r * rb, rb), cs] = _pool_block(
                    extp_ref, r * rb, cs, w, pos).astype(BF16)

        yn = _layer_norm_f32(y_ref[...], lg_ref[...], lb_ref[...])
        cat_ref[:, 0:dc] = (yn * jax.nn.sigmoid(yn)).astype(BF16)
        for gi in range(len(POOL_WINDOWS)):
            gs = slice(gi * pg, (gi + 1) * pg)
            pb = _dot(pooled_ref[:, gs], pw_ref[gi].astype(BF16)) * ps_ref[:, gs]
            cat_ref[:, dc + gi * pg: dc + (gi + 1) * pg] = pb.astype(BF16)

    n = pl.program_id(1)

    @pl.when(i == 0)
    def _():
        woc_ref[n] = wo_ref[...].astype(BF16)

    o_ref[...] = x_ref[...] + _dot(cat_ref[...], woc_ref[n])


def _ab_out(x2d, a, p, li, conv_w, conv_b, ln_g, ln_b, pool_w, pool_scale, w_out, *,
            seq, bm=512, bn=512):
    m, d = x2d.shape
    dc = a.shape[1]
    hb = bm // HALO
    assert seq % bm == 0 and bm % ROW_BLOCK == 0 and HALO >= CONV_WIDTH - 1
    assert HALO >= max(POOL_WINDOWS) and (dc // len(POOL_WINDOWS)) % LANES == 0
    kern = functools.partial(_ab_out_kernel, bm=bm, tiles_per_seq=seq // bm)
    halo_map = lambda i, n: (jnp.maximum(i * hb - 1, 0), 0)
    return pl.pallas_call(
        kern,
        grid=(m // bm, d // bn),
        in_specs=[
            pl.BlockSpec((bm, dc), lambda i, n: (i, 0)),
            pl.BlockSpec((HALO, dc), halo_map),
            pl.BlockSpec((bm, dc), lambda i, n: (i, 0)),
            pl.BlockSpec((HALO, dc), halo_map),
            pl.BlockSpec((None,) + conv_w.shape[1:], lambda i, n: (li, 0, 0)),
            _row_spec(li, dc),
            _row_spec(li, dc),
            _row_spec(li, dc),
            pl.BlockSpec((None,) + pool_w.shape[1:], lambda i, n: (li, 0, 0, 0)),
            _row_spec(li, dc),
            pl.BlockSpec((bm, bn), lambda i, n: (i, n)),
            pl.BlockSpec((None, 2 * dc, bn), _cached_col_map(li, 0, d // bn)),
        ],
        out_specs=pl.BlockSpec((bm, bn), lambda i, n: (i, n)),
        out_shape=jax.ShapeDtypeStruct((m, d), F32),
        scratch_shapes=[pltpu.VMEM((bm + HALO, dc), F32),
                        pltpu.VMEM((bm + HALO, dc), F32),
                        pltpu.VMEM((bm, dc), F32),
                        pltpu.VMEM((bm, dc), BF16),
                        pltpu.VMEM((bm, 2 * dc), BF16),
                        pltpu.VMEM((d // bn, 2 * dc, bn), BF16)],
        compiler_params=_cparams(("arbitrary", "arbitrary")),
        name="ab_out",
    )(a, a, p, p, conv_w, _rows3(conv_b), _rows3(ln_g), _rows3(ln_b), pool_w,
      _rows3(pool_scale), x2d, w_out)


def _sgu_in_kernel(x_ref, g_ref, w_ref, lg_ref, lb_ref, u_ref, vn_ref,
                   h_ref, v_ref, mu_ref, sc_ref):
    j = pl.program_id(1)
    nh, _, bn = v_ref.shape

    @pl.when(j == 0)
    def _():
        h_ref[...] = _rms_norm_f32(x_ref[...], g_ref[...]).astype(BF16)

    def z():
        return jax.nn.gelu(_dot(h_ref[...], w_ref[...].astype(BF16)), approximate=True)

    @pl.when(j < nh)
    def _():
        v_ref[j] = z()

    def u_step(c):
        u_ref[...] = z().astype(BF16)
        if c == 0:
            width = nh * bn
            total = v_ref[0].sum(axis=-1, keepdims=True)
            for k in range(1, nh):
                total = total + v_ref[k].sum(axis=-1, keepdims=True)
            mu = total / width
            sq = None
            for k in range(nh):
                xc = v_ref[k] - mu
                part = (xc * xc).sum(axis=-1, keepdims=True)
                sq = part if sq is None else sq + part
            mu_ref[...] = mu
            sc_ref[...] = lax.rsqrt(sq / width + EPS)
        cs = slice(c * bn, (c + 1) * bn)
        vn = (v_ref[c] - mu_ref[...]) * sc_ref[...] * lg_ref[:, cs] + lb_ref[:, cs]
        vn_ref[:, cs] = vn.astype(BF16)

    for c in range(nh):
        pl.when(j == nh + c)(functools.partial(u_step, c))


def _sgu_in(x2d, layer, norm_g, li, w_in, ln_g, ln_b, *, bm=1024, bn=512):
    m, d = x2d.shape
    ds = w_in.shape[2] // 2
    nh = ds // bn
    return pl.pallas_call(
        _sgu_in_kernel,
        grid=(m // bm, 2 * nh),
        in_specs=[
            pl.BlockSpec((bm, d), lambda i, j: (i, 0)),
            _row_spec(layer, d),
            pl.BlockSpec((None, d, bn), lambda i, j: (li, 0, (j + nh) % (2 * nh))),
            _row_spec(li, ds),
            _row_spec(li, ds),
        ],
        out_specs=[pl.BlockSpec((bm, bn), lambda i, j: (i, jnp.maximum(j - nh, 0))),
                   pl.BlockSpec((bm, ds), lambda i, j: (i, 0))],
        out_shape=[jax.ShapeDtypeStruct((m, ds), BF16),
                   jax.ShapeDtypeStruct((m, ds), BF16)],
        scratch_shapes=[pltpu.VMEM((bm, d), BF16),
                        pltpu.VMEM((nh, bm, bn), F32),
                        pltpu.VMEM((bm, 1), F32),
                        pltpu.VMEM((bm, 1), F32)],
        compiler_params=_cparams(("parallel", "arbitrary")),
        name="sgu_in",
    )(x2d, _rows3(norm_g), w_in, _rows3(ln_g), _rows3(ln_b))


def _sgu_out_kernel(u_ref, vn_ref, sw_ref, sbt_ref, x_ref, wo_ref, o_ref, y_ref, woc_ref,
                    *, bm):
    n = pl.program_id(1)
    ds = vn_ref.shape[1]
    hd = ds // SGU_HEADS

    @pl.when(pl.program_id(1) == 0)
    def _():
        ri = lax.broadcasted_iota(jnp.int32, (SGU_LEN, SGU_LEN), 0) // CHUNK
        ci = lax.broadcasted_iota(jnp.int32, (SGU_LEN, SGU_LEN), 1) // CHUNK
        mask = ci <= ri
        for g in range(SGU_HEADS):
            wg = jnp.where(mask, sw_ref[g], 0.0).astype(BF16)
            bias = sbt_ref[:, g:g + 1]
            gs = slice(g * hd, (g + 1) * hd)
            for w in range(bm // SGU_LEN):
                rs = pl.ds(w * SGU_LEN, SGU_LEN)
                mixed = _dot(wg, vn_ref[rs, gs]) + bias
                y_ref[rs, gs] = (u_ref[rs, gs].astype(F32) * mixed).astype(BF16)

    @pl.when(pl.program_id(0) == 0)
    def _():
        woc_ref[n] = wo_ref[...].astype(BF16)

    o_ref[...] = x_ref[...] + _dot(y_ref[...], woc_ref[n])


def _sgu_out(x2d, u, vn, li, sgu_w, sgu_b, w_out, *, bm=1024, bn=512):
    m, d = x2d.shape
    ds = vn.shape[1]
    kern = functools.partial(_sgu_out_kernel, bm=bm)
    return pl.pallas_call(
        kern,
        grid=(m // bm, d // bn),
        in_specs=[
            pl.BlockSpec((bm, ds), lambda i, n: (i, 0)),
            pl.BlockSpec((bm, ds), lambda i, n: (i, 0)),
            pl.BlockSpec((None,) + sgu_w.shape[1:], lambda i, n: (li, 0, 0, 0)),
            pl.BlockSpec((None, SGU_LEN, SGU_HEADS), lambda i, n: (li, 0, 0)),
            pl.BlockSpec((bm, bn), lambda i, n: (i, n)),
            pl.BlockSpec((None, ds, bn), _cached_col_map(li, 0, d // bn)),
        ],
        out_specs=pl.BlockSpec((bm, bn), lambda i, n: (i, n)),
        out_shape=jax.ShapeDtypeStruct((m, d), F32),
        scratch_shapes=[pltpu.VMEM((bm, ds), BF16),
                        pltpu.VMEM((d // bn, ds, bn), BF16)],
        compiler_params=_cparams(("arbitrary", "arbitrary")),
        name="sgu_out",
    )(u, vn, sgu_w, jnp.swapaxes(sgu_b, 1, 2), x2d, w_out)


def kernel(x, norm_ffn1, ffn1_w_gate, ffn1_w_up, ffn1_w_down, norm_mix, norm_ffn2,
           ffn2_w_gate, ffn2_w_up, ffn2_w_down, ab_w_in, conv_w, conv_b, conv_ln_g,
           conv_ln_b, pool_w, pool_scale, ab_w_out, c_w_in, sgu_ln_g, sgu_ln_b, sgu_w,
           sgu_b, c_w_out, final_norm):
    bsz, seq, d = x.shape
    depth = norm_ffn1.shape[0]
    x2d = x.reshape(bsz * seq, d)
    for layer in range(depth):
        x2d = _ffn(x2d, layer, norm_ffn1, ffn1_w_gate, ffn1_w_up, ffn1_w_down,
                   final_norm, final_norm=False)
        i = layer // 2
        if layer % 2 == 0:
            a, p = _ab_in(x2d, layer, norm_mix, i, ab_w_in)
            x2d = _ab_out(x2d, a, p, i, conv_w, conv_b, conv_ln_g, conv_ln_b,
                          pool_w, pool_scale, ab_w_out, seq=seq)
        else:
            u, vn = _sgu_in(x2d, layer, norm_mix, i, c_w_in, sgu_ln_g, sgu_ln_b)
            x2d = _sgu_out(x2d, u, vn, i, sgu_w, sgu_b, c_w_out)
        x2d = _ffn(x2d, layer, norm_ffn2, ffn2_w_gate, ffn2_w_up, ffn2_w_down,
                   final_norm, final_norm=(layer == depth - 1))
    return x2d.reshape(bsz, seq, d)
```

```python
import functools

import jax
import jax.numpy as jnp
from jax import lax
from jax.experimental import pallas as pl
from jax.experimental.pallas import tpu as pltpu

EPS = 1e-6
CHUNK = 64
CONV_WIDTH = 31
POOL_WINDOWS = (2, 4, 8, 16)
SGU_LEN = 128
SGU_HEADS = 8
HALO = 32
SUBLANES = 8
LANES = 128
ROW_BLOCK = 64
ROW_SLAB = 256
BF16 = jnp.bfloat16
F32 = jnp.float32

VMEM_LIMIT_BYTES = 58 * 1024 * 1024


def _cparams(semantics):
    return pltpu.CompilerParams(dimension_semantics=semantics,
                                vmem_limit_bytes=VMEM_LIMIT_BYTES)


def _rms_norm_f32(xf, g):
    ms = jnp.mean(xf * xf, axis=-1, keepdims=True)
    return xf * lax.rsqrt(ms + EPS) * g


def _layer_norm_f32(xf, g, b):
    mu = jnp.mean(xf, axis=-1, keepdims=True)
    xc = xf - mu
    var = jnp.mean(xc * xc, axis=-1, keepdims=True)
    return xc * lax.rsqrt(var + EPS) * g + b


def _dot(a, b):
    return jnp.dot(a, b, preferred_element_type=F32)


def _rows3(a):
    return a.reshape(a.shape[0], 1, a.shape[1])


def _row_spec(layer, c):
    return pl.BlockSpec((None, 1, c), lambda *_: (layer, 0, 0))


def _ffn_kernel(x_hbm, g_ref, wg_ref, wu_ref, wd_ref, fg_ref, o_ref, h_ref, xbuf, sem, *,
                bm, f_total, bf, final_norm):
    i = pl.program_id(0)
    j = pl.program_id(1)
    ni = pl.num_programs(0)
    nj = pl.num_programs(1)

    def x_copy(tile):
        return pltpu.make_async_copy(x_hbm.at[pl.ds(tile * bm, bm), :], xbuf, sem)

    @pl.when((i == 0) & (j == 0))
    def _():
        x_copy(0).start()

    @pl.when((j == 1) & (i + 1 < ni))
    def _():
        x_copy(i + 1).start()

    def step(width, first=False, last=False):
        if first:
            x_copy(i).wait()
            h_ref[...] = _rms_norm_f32(xbuf[...], g_ref[...]).astype(BF16)
        h = h_ref[...]
        g = _dot(h, wg_ref[:, :width].astype(BF16))
        u = _dot(h, wu_ref[:, :width].astype(BF16))
        a = (g * jax.nn.sigmoid(g) * u * 0.5).astype(BF16)
        down = _dot(a, wd_ref[:width, :].astype(BF16))
        if first:
            o_ref[...] = xbuf[...] + down
        else:
            o_ref[...] += down
        if last and final_norm:
            o_ref[...] = _rms_norm_f32(o_ref[...], fg_ref[...])

    rem = f_total % bf
    pl.when(j == 0)(lambda: step(bf, first=True))
    pl.when((j > 0) & (j < nj - 1))(lambda: step(bf))
    pl.when(j == nj - 1)(lambda: step(rem if rem else bf, last=True))


def _ffn(x2d, layer, norm_g, wg, wu, wd, final_g, *, final_norm, bm=1024, bf=512):
    m, d = x2d.shape
    f = wg.shape[2]
    nj = pl.cdiv(f, bf)
    assert nj >= 3 and m % bm == 0
    kern = functools.partial(_ffn_kernel, bm=bm, f_total=f, bf=bf, final_norm=final_norm)
    return pl.pallas_call(
        kern,
        grid=(m // bm, nj),
        in_specs=[
            pl.BlockSpec(memory_space=pl.ANY),
            _row_spec(layer, d),
            pl.BlockSpec((None, d, bf), lambda i, j: (layer, 0, j)),
            pl.BlockSpec((None, d, bf), lambda i, j: (layer, 0, j)),
            pl.BlockSpec((None, bf, d), lambda i, j: (layer, j, 0)),
            pl.BlockSpec((1, d), lambda i, j: (0, 0)),
        ],
        out_specs=pl.BlockSpec((bm, d), lambda i, j: (i, 0)),
        out_shape=jax.ShapeDtypeStruct((m, d), F32),
        scratch_shapes=[pltpu.VMEM((bm, d), BF16),
                        pltpu.VMEM((bm, d), F32),
                        pltpu.SemaphoreType.DMA(())],
        compiler_params=_cparams(("arbitrary", "arbitrary")),
        name="ffn_final" if final_norm else "ffn",
    )(x2d, _rows3(norm_g), wg, wu, wd, final_g.reshape(1, d))


def _cached_col_map(layer, first_block, nb):
    return lambda i, j: (layer, 0, first_block + jnp.where(i == 0, j, nb - 1))


def _conv_block(ext_ref, cw_ref, cb_ref, r0, cs):
    rb = ROW_BLOCK
    base = HALO - (CONV_WIDTH - 1)
    e = ext_ref[pl.ds(r0, rb + HALO), cs]
    acc = jnp.broadcast_to(cb_ref[:, cs], (rb, LANES))
    for r in range(SUBLANES):
        qs = [q for q in range((base + CONV_WIDTH - 1) // SUBLANES + 1)
              if 0 <= SUBLANES * q + r - base < CONV_WIDTH]
        s = e if r == 0 else pltpu.roll(e, rb + HALO - r, 0)
        for q in qs:
            k = SUBLANES * q + r - base
            acc = acc + s[SUBLANES * q:SUBLANES * q + rb] * cw_ref[k:k + 1, cs]
    return acc


def _pool_sums(ext_ref, r0, cs):
    rb = ROW_BLOCK
    x = ext_ref[pl.ds(r0, rb + HALO), cs]
    tok = x[HALO:HALO + rb]
    s = x
    start, width = 0, 1
    sums = []
    while width < max(POOL_WINDOWS):
        new_start = start + SUBLANES
        n = rb + HALO - new_start
        off = new_start - start
        s = s[off:off + n] + s[off - width:off - width + n]
        start, width = new_start, 2 * width
        sums.append(s[HALO - start:HALO - start + rb])
    return tok, sums


def _ab_in_kernel(x_ref, g_ref, wa_ref, wgate_ref, wp_ref, cw_ref, cb_ref, lg_ref, lb_ref,
                  pw_ref, ps_ref, ao_ref, pb_ref,
                  h_ref, exta_ref, extp_ref, halo_a_ref, halo_p_ref, y_ref, pooled_ref, *,
                  bm, tiles_per_seq):
    i = pl.program_id(0)
    j = pl.program_id(1)
    nb, _, bn = y_ref.shape
    rb = ROW_BLOCK

    @pl.when(i == 0)
    def _():
        halo_a_ref[j] = jnp.zeros((HALO, bn), F32)
        halo_p_ref[j] = jnp.zeros((HALO, bn), F32)

    t_in_seq = i % tiles_per_seq
    seq_start = t_in_seq == 0
    window = jnp.left_shift(2, j)

    def body(first, last):
        wa = wa_ref[...].astype(BF16)
        wgate = wgate_ref[...].astype(BF16)
        wp = wp_ref[...].astype(BF16)
        pw = pw_ref[...].astype(BF16)
        exta_ref[0:HALO, :] = jnp.where(seq_start, 0.0, halo_a_ref[j])
        extp_ref[0:HALO, :] = jnp.where(seq_start, 0.0, halo_p_ref[j])
        def project(q):
            rows = pl.ds(q * ROW_SLAB, ROW_SLAB)
            if first:
                h_ref[rows, :] = _rms_norm_f32(x_ref[rows, :], g_ref[...]).astype(BF16)
            h = h_ref[rows, :]
            return _dot(h, wa) * jax.nn.sigmoid(_dot(h, wgate)), _dot(h, wp)

        nslab = bm // ROW_SLAB
        ahead = project(0)
        for q in range(nslab):
            q0 = q * ROW_SLAB
            rows = pl.ds(q0, ROW_SLAB)
            exta_ref[pl.ds(HALO + q0, ROW_SLAB), :] = ahead[0]
            extp_ref[pl.ds(HALO + q0, ROW_SLAB), :] = ahead[1]
            if q + 1 < nslab:
                ahead = project(q + 1)

            for c in range(bn // LANES):
                cs = slice(c * LANES, (c + 1) * LANES)
                for r0 in range(q0, q0 + ROW_SLAB, rb):
                    y_ref[j, pl.ds(r0, rb), cs] = _conv_block(exta_ref, cw_ref, cb_ref, r0, cs)
                    tok, sums = _pool_sums(extp_ref, r0, cs)
                    s = sums[0]
                    for k in range(1, len(sums)):
                        s = jnp.where(j == k, sums[k], s)
                    pos = t_in_seq * bm + r0 + lax.broadcasted_iota(jnp.int32, (rb, 1), 0)
                    cnt = jnp.minimum(pos + 1, window).astype(F32)
                    pooled_ref[pl.ds(r0, rb), cs] = (s / cnt - tok).astype(BF16)
            pb = _dot(pooled_ref[rows, :], pw) * ps_ref[...]
            pb_ref[rows, :] = pb.astype(BF16)

            if last:
                width = nb * bn
                total = y_ref[0, rows, :].sum(axis=-1, keepdims=True)
                for c in range(1, nb):
                    total = total + y_ref[c, rows, :].sum(axis=-1, keepdims=True)
                mu = total / width
                sq = None
                for c in range(nb):
                    yc = y_ref[c, rows, :] - mu
                    part = (yc * yc).sum(axis=-1, keepdims=True)
                    sq = part if sq is None else sq + part
                scale = lax.rsqrt(sq / width + EPS)
                for c in range(nb):
                    cs = slice(c * bn, (c + 1) * bn)
                    yn = (y_ref[c, rows, :] - mu) * scale * lg_ref[:, cs] + lb_ref[:, cs]
                    ao_ref[rows, cs] = (yn * jax.nn.sigmoid(yn)).astype(BF16)
        halo_a_ref[j] = exta_ref[bm:bm + HALO, :]
        halo_p_ref[j] = extp_ref[bm:bm + HALO, :]

    pl.when(j == 0)(lambda: body(True, False))
    pl.when((j > 0) & (j < nb - 1))(lambda: body(False, False))
    pl.when(j == nb - 1)(lambda: body(False, True))


def _ab_out_kernel(ao_ref, pb_ref, x_ref, wo_ref, o_ref, woc_ref):
    i = pl.program_id(0)
    n = pl.program_id(1)
    dc = ao_ref.shape[1]

    @pl.when(i == 0)
    def _():
        woc_ref[n] = wo_ref[...].astype(BF16)

    o_ref[...] = (x_ref[...] + _dot(ao_ref[...], woc_ref[n, 0:dc, :])
                  + _dot(pb_ref[...], woc_ref[n, dc:2 * dc, :]))


def _ab_in(x2d, layer, norm_g, li, w_in, conv_w, conv_b, ln_g, ln_b, pool_w, pool_scale, *,
           seq, bm=1024):
    m, d = x2d.shape
    dc = w_in.shape[2] // 3
    bn = dc // len(POOL_WINDOWS)
    nb = dc // bn
    assert seq % bm == 0 and bm % ROW_SLAB == 0 and ROW_SLAB % ROW_BLOCK == 0
    assert bn % LANES == 0 and nb >= 3
    assert HALO >= CONV_WIDTH - 1 and HALO >= max(POOL_WINDOWS)
    assert POOL_WINDOWS == tuple(2 << k for k in range(nb))
    kern = functools.partial(_ab_in_kernel, bm=bm, tiles_per_seq=seq // bm)
    col = lambda first: (lambda i, j: (li, 0, first + j))
    return pl.pallas_call(
        kern,
        grid=(m // bm, nb),
        in_specs=[
            pl.BlockSpec((bm, d), lambda i, j: (i, 0)),
            _row_spec(layer, d),
            pl.BlockSpec((None, d, bn), col(0)),
            pl.BlockSpec((None, d, bn), col(nb)),
            pl.BlockSpec((None, d, bn), col(2 * nb)),
            pl.BlockSpec((None, CONV_WIDTH, bn), col(0)),
            pl.BlockSpec((None, 1, bn), col(0)),
            _row_spec(li, dc),
            _row_spec(li, dc),
            pl.BlockSpec((None, None, bn, bn), lambda i, j: (li, j, 0, 0)),
            pl.BlockSpec((None, 1, bn), col(0)),
        ],
        out_specs=[pl.BlockSpec((bm, dc), lambda i, j: (i, 0)),
                   pl.BlockSpec((bm, bn), lambda i, j: (i, j))],
        out_shape=[jax.ShapeDtypeStruct((m, dc), BF16),
                   jax.ShapeDtypeStruct((m, dc), BF16)],
        scratch_shapes=[pltpu.VMEM((bm, d), BF16),
                        pltpu.VMEM((bm + HALO, bn), F32),
                        pltpu.VMEM((bm + HALO, bn), F32),
                        pltpu.VMEM((nb, HALO, bn), F32),
                        pltpu.VMEM((nb, HALO, bn), F32),
                        pltpu.VMEM((nb, bm, bn), F32),
                        pltpu.VMEM((bm, bn), BF16)],
        compiler_params=_cparams(("arbitrary", "arbitrary")),
        name="ab_in",
    )(x2d, _rows3(norm_g), w_in, w_in, w_in, conv_w, _rows3(conv_b), _rows3(ln_g),
      _rows3(ln_b), pool_w, _rows3(pool_scale))


def _ab_out(x2d, ao, pb, li, w_out, *, bm=1024, bn=512):
    m, d = x2d.shape
    dc = ao.shape[1]
    return pl.pallas_call(
        _ab_out_kernel,
        grid=(m // bm, d // bn),
        in_specs=[
            pl.BlockSpec((bm, dc), lambda i, n: (i, 0)),
            pl.BlockSpec((bm, dc), lambda i, n: (i, 0)),
            pl.BlockSpec((bm, bn), lambda i, n: (i, n)),
            pl.BlockSpec((None, 2 * dc, bn), _cached_col_map(li, 0, d // bn)),
        ],
        out_specs=pl.BlockSpec((bm, bn), lambda i, n: (i, n)),
        out_shape=jax.ShapeDtypeStruct((m, d), F32),
        scratch_shapes=[pltpu.VMEM((d // bn, 2 * dc, bn), BF16)],
        compiler_params=_cparams(("arbitrary", "arbitrary")),
        name="ab_out",
    )(ao, pb, x2d, w_out)


def _sgu_in_kernel(x_ref, g_ref, w_ref, lg_ref, lb_ref, u_ref, vn_ref,
                   h_ref, v_ref, mu_ref, sc_ref):
    j = pl.program_id(1)
    nh, _, bn = v_ref.shape

    @pl.when(j == 0)
    def _():
        h_ref[...] = _rms_norm_f32(x_ref[...], g_ref[...]).astype(BF16)

    def z():
        return jax.nn.gelu(_dot(h_ref[...], w_ref[...].astype(BF16)), approximate=True)

    @pl.when(j < nh)
    def _():
        v_ref[j] = z()

    def u_step(c):
        u_ref[...] = z().astype(BF16)
        if c == 0:
            width = nh * bn
            total = v_ref[0].sum(axis=-1, keepdims=True)
            for k in range(1, nh):
                total = total + v_ref[k].sum(axis=-1, keepdims=True)
            mu = total / width
            sq = None
            for k in range(nh):
                xc = v_ref[k] - mu
                part = (xc * xc).sum(axis=-1, keepdims=True)
                sq = part if sq is None else sq + part
            mu_ref[...] = mu
            sc_ref[...] = lax.rsqrt(sq / width + EPS)
        cs = slice(c * bn, (c + 1) * bn)
        vn = (v_ref[c] - mu_ref[...]) * sc_ref[...] * lg_ref[:, cs] + lb_ref[:, cs]
        vn_ref[:, cs] = vn.astype(BF16)

    for c in range(nh):
        pl.when(j == nh + c)(functools.partial(u_step, c))


def _sgu_in(x2d, layer, norm_g, li, w_in, ln_g, ln_b, *, bm=1024, bn=512):
    m, d = x2d.shape
    ds = w_in.shape[2] // 2
    nh = ds // bn
    return pl.pallas_call(
        _sgu_in_kernel,
        grid=(m // bm, 2 * nh),
        in_specs=[
            pl.BlockSpec((bm, d), lambda i, j: (i, 0)),
            _row_spec(layer, d),
            pl.BlockSpec((None, d, bn), lambda i, j: (li, 0, (j + nh) % (2 * nh))),
            _row_spec(li, ds),
            _row_spec(li, ds),
        ],
        out_specs=[pl.BlockSpec((bm, bn), lambda i, j: (i, jnp.maximum(j - nh, 0))),
                   pl.BlockSpec((bm, ds), lambda i, j: (i, 0))],
        out_shape=[jax.ShapeDtypeStruct((m, ds), BF16),
                   jax.ShapeDtypeStruct((m, ds), BF16)],
        scratch_shapes=[pltpu.VMEM((bm, d), BF16),
                        pltpu.VMEM((nh, bm, bn), F32),
                        pltpu.VMEM((bm, 1), F32),
                        pltpu.VMEM((bm, 1), F32)],
        compiler_params=_cparams(("parallel", "arbitrary")),
        name="sgu_in",
    )(x2d, _rows3(norm_g), w_in, _rows3(ln_g), _rows3(ln_b))


def _sgu_out_kernel(u_ref, vn_ref, sw_ref, sbt_ref, x_ref, wo_ref, o_ref, y_ref, woc_ref,
                    *, bm):
    n = pl.program_id(1)
    ds = vn_ref.shape[1]
    hd = ds // SGU_HEADS

    @pl.when(pl.program_id(1) == 0)
    def _():
        ri = lax.broadcasted_iota(jnp.int32, (SGU_LEN, SGU_LEN), 0) // CHUNK
        ci = lax.broadcasted_iota(jnp.int32, (SGU_LEN, SGU_LEN), 1) // CHUNK
        mask = ci <= ri
        for g in range(SGU_HEADS):
            wg = jnp.where(mask, sw_ref[g], 0.0).astype(BF16)
            bias = sbt_ref[:, g:g + 1]
            gs = slice(g * hd, (g + 1) * hd)
            for w in range(bm // SGU_LEN):
                rs = pl.ds(w * SGU_LEN, SGU_LEN)
                mixed = _dot(wg, vn_ref[rs, gs]) + bias
                y_ref[rs, gs] = (u_ref[rs, gs].astype(F32) * mixed).astype(BF16)

    @pl.when(pl.program_id(0) == 0)
    def _():
        woc_ref[n] = wo_ref[...].astype(BF16)

    o_ref[...] = x_ref[...] + _dot(y_ref[...], woc_ref[n])


def _sgu_out(x2d, u, vn, li, sgu_w, sgu_b, w_out, *, bm=1024, bn=512):
    m, d = x2d.shape
    ds = vn.shape[1]
    kern = functools.partial(_sgu_out_kernel, bm=bm)
    return pl.pallas_call(
        kern,
        grid=(m // bm, d // bn),
        in_specs=[
            pl.BlockSpec((bm, ds), lambda i, n: (i, 0)),
            pl.BlockSpec((bm, ds), lambda i, n: (i, 0)),
            pl.BlockSpec((None,) + sgu_w.shape[1:], lambda i, n: (li, 0, 0, 0)),
            pl.BlockSpec((None, SGU_LEN, SGU_HEADS), lambda i, n: (li, 0, 0)),
            pl.BlockSpec((bm, bn), lambda i, n: (i, n)),
            pl.BlockSpec((None, ds, bn), _cached_col_map(li, 0, d // bn)),
        ],
        out_specs=pl.BlockSpec((bm, bn), lambda i, n: (i, n)),
        out_shape=jax.ShapeDtypeStruct((m, d), F32),
        scratch_shapes=[pltpu.VMEM((bm, ds), BF16),
                        pltpu.VMEM((d // bn, ds, bn), BF16)],
        compiler_params=_cparams(("arbitrary", "arbitrary")),
        name="sgu_out",
    )(u, vn, sgu_w, jnp.swapaxes(sgu_b, 1, 2), x2d, w_out)


def kernel(x, norm_ffn1, ffn1_w_gate, ffn1_w_up, ffn1_w_down, norm_mix, norm_ffn2,
           ffn2_w_gate, ffn2_w_up, ffn2_w_down, ab_w_in, conv_w, conv_b, conv_ln_g,
           conv_ln_b, pool_w, pool_scale, ab_w_out, c_w_in, sgu_ln_g, sgu_ln_b, sgu_w,
           sgu_b, c_w_out, final_norm):
    bsz, seq, d = x.shape
    depth = norm_ffn1.shape[0]
    x2d = x.reshape(bsz * seq, d)
    for layer in range(depth):
        x2d = _ffn(x2d, layer, norm_ffn1, ffn1_w_gate, ffn1_w_up, ffn1_w_down,
                   final_norm, final_norm=False)
        i = layer // 2
        if layer % 2 == 0:
            ao, pb = _ab_in(x2d, layer, norm_mix, i, ab_w_in, conv_w, conv_b, conv_ln_g,
                            conv_ln_b, pool_w, pool_scale, seq=seq)
            x2d = _ab_out(x2d, ao, pb, i, ab_w_out)
        else:
            u, vn = _sgu_in(x2d, layer, norm_mix, i, c_w_in, sgu_ln_g, sgu_ln_b)
            x2d = _sgu_out(x2d, u, vn, i, sgu_w, sgu_b, c_w_out)
        x2d = _ffn(x2d, layer, norm_ffn2, ffn2_w_gate, ffn2_w_up, ffn2_w_down,
                   final_norm, final_norm=(layer == depth - 1))
    return x2d.reshape(bsz, seq, d)
```

```python
import functools

import jax
import jax.numpy as jnp
from jax import lax
from jax.experimental import pallas as pl
from jax.experimental.pallas import tpu as pltpu

EPS = 1e-6
CHUNK = 64
CONV_WIDTH = 31
POOL_WINDOWS = (2, 4, 8, 16)
SGU_LEN = 128
SGU_HEADS = 8
HALO = 32
SUBLANES = 8
LANES = 128
ROW_BLOCK = 32
ROW_SLAB = 256
BF16 = jnp.bfloat16
F32 = jnp.float32

VMEM_LIMIT_BYTES = 58 * 1024 * 1024


def _cparams(semantics):
    return pltpu.CompilerParams(dimension_semantics=semantics,
                                vmem_limit_bytes=VMEM_LIMIT_BYTES)


def _rms_norm_f32(xf, g):
    ms = jnp.mean(xf * xf, axis=-1, keepdims=True)
    return xf * lax.rsqrt(ms + EPS) * g


def _layer_norm_f32(xf, g, b):
    mu = jnp.mean(xf, axis=-1, keepdims=True)
    xc = xf - mu
    var = jnp.mean(xc * xc, axis=-1, keepdims=True)
    return xc * lax.rsqrt(var + EPS) * g + b


def _dot(a, b):
    return jnp.dot(a, b, preferred_element_type=F32)


def _rows3(a):
    return a.reshape(a.shape[0], 1, a.shape[1])


def _row_spec(layer, c):
    return pl.BlockSpec((None, 1, c), lambda *_: (layer, 0, 0))


def _ffn_kernel(x_hbm, g_ref, wg_ref, wu_ref, wd_ref, fg_ref, o_ref, h_ref, xbuf, sem, *,
                bm, f_total, bf, final_norm):
    i = pl.program_id(0)
    j = pl.program_id(1)
    ni = pl.num_programs(0)
    nj = pl.num_programs(1)

    def x_copy(tile):
        return pltpu.make_async_copy(x_hbm.at[pl.ds(tile * bm, bm), :], xbuf, sem)

    @pl.when((i == 0) & (j == 0))
    def _():
        x_copy(0).start()

    @pl.when((j == 1) & (i + 1 < ni))
    def _():
        x_copy(i + 1).start()

    def step(width, first=False, last=False):
        if first:
            x_copy(i).wait()
            h_ref[...] = _rms_norm_f32(xbuf[...], g_ref[...]).astype(BF16)
        h = h_ref[...]

        def swiglu(g, u):
            return (g * jax.nn.sigmoid(g) * u * 0.5).astype(BF16)

        g = _dot(h, wg_ref[:, :width].astype(BF16))
        u = _dot(h, wu_ref[:, :width].astype(BF16))
        down = _dot(swiglu(g, u), wd_ref[:width, :].astype(BF16))
        if first:
            o_ref[...] = xbuf[...] + down
        else:
            o_ref[...] += down
        if last and final_norm:
            o_ref[...] = _rms_norm_f32(o_ref[...], fg_ref[...])

    rem = f_total % bf
    pl.when(j == 0)(lambda: step(bf, first=True))
    pl.when((j > 0) & (j < nj - 1))(lambda: step(bf))
    pl.when(j == nj - 1)(lambda: step(rem if rem else bf, last=True))


def _ffn(x2d, layer, norm_g, wg, wu, wd, final_g, *, final_norm, bm=1024, bf=512):
    m, d = x2d.shape
    f = wg.shape[2]
    nj = pl.cdiv(f, bf)
    assert nj >= 3 and m % bm == 0
    kern = functools.partial(_ffn_kernel, bm=bm, f_total=f, bf=bf, final_norm=final_norm)
    return pl.pallas_call(
        kern,
        grid=(m // bm, nj),
        in_specs=[
            pl.BlockSpec(memory_space=pl.ANY),
            _row_spec(layer, d),
            pl.BlockSpec((None, d, bf), lambda i, j: (layer, 0, j)),
            pl.BlockSpec((None, d, bf), lambda i, j: (layer, 0, j)),
            pl.BlockSpec((None, bf, d), lambda i, j: (layer, j, 0)),
            pl.BlockSpec((1, d), lambda i, j: (0, 0)),
        ],
        out_specs=pl.BlockSpec((bm, d), lambda i, j: (i, 0)),
        out_shape=jax.ShapeDtypeStruct((m, d), F32),
        scratch_shapes=[pltpu.VMEM((bm, d), BF16),
                        pltpu.VMEM((bm, d), F32),
                        pltpu.SemaphoreType.DMA(())],
        compiler_params=_cparams(("arbitrary", "arbitrary")),
        name="ffn_final" if final_norm else "ffn",
    )(x2d, _rows3(norm_g), wg, wu, wd, final_g.reshape(1, d))


def _cached_col_map(layer, first_block, nb):
    return lambda i, j: (layer, 0, first_block + jnp.where(i == 0, j, nb - 1))


def _conv_block(ext_ref, cw_ref, cb_ref, r0, cs):
    rb = ROW_BLOCK
    base = HALO - (CONV_WIDTH - 1)
    e = ext_ref[pl.ds(r0, rb + HALO), cs]
    acc = jnp.broadcast_to(cb_ref[:, cs], (rb, LANES))
    for r in range(SUBLANES):
        qs = [q for q in range((base + CONV_WIDTH - 1) // SUBLANES + 1)
              if 0 <= SUBLANES * q + r - base < CONV_WIDTH]
        s = e if r == 0 else pltpu.roll(e, rb + HALO - r, 0)
        for q in qs:
            k = SUBLANES * q + r - base
            acc = acc + s[SUBLANES * q:SUBLANES * q + rb] * cw_ref[k:k + 1, cs]
    return acc


def _pool_sums(ext_ref, r0, cs):
    rb = ROW_BLOCK
    x = ext_ref[pl.ds(r0, rb + HALO), cs]
    tok = x[HALO:HALO + rb]
    s = x
    start, width = 0, 1
    sums = []
    while width < max(POOL_WINDOWS):
        new_start = start + SUBLANES
        n = rb + HALO - new_start
        off = new_start - start
        s = s[off:off + n] + s[off - width:off - width + n]
        start, width = new_start, 2 * width
        sums.append(s[HALO - start:HALO - start + rb])
    return tok, sums


def _ab_in_kernel(x_ref, g_ref, wa_ref, wgate_ref, wp_ref, cw_ref, cb_ref, lg_ref, lb_ref,
                  pw_ref, ps_ref, ao_ref, pb_ref,
                  h_ref, exta_ref, extp_ref, halo_a_ref, halo_p_ref, y_ref, pooled_ref, *,
                  bm, tiles_per_seq):
    i = pl.program_id(0)
    j = pl.program_id(1)
    nb, _, bn = y_ref.shape
    rb = ROW_BLOCK

    @pl.when(i == 0)
    def _():
        halo_a_ref[j] = jnp.zeros((HALO, bn), F32)
        halo_p_ref[j] = jnp.zeros((HALO, bn), F32)

    t_in_seq = i % tiles_per_seq
    seq_start = t_in_seq == 0
    window = jnp.left_shift(2, j)

    def body(first, last):
        wa = wa_ref[...].astype(BF16)
        wgate = wgate_ref[...].astype(BF16)
        wp = wp_ref[...].astype(BF16)
        pw = pw_ref[...].astype(BF16)
        exta_ref[0:HALO, :] = jnp.where(seq_start, 0.0, halo_a_ref[j])
        extp_ref[0:HALO, :] = jnp.where(seq_start, 0.0, halo_p_ref[j])
        def project(q):
            rows = pl.ds(q * ROW_SLAB, ROW_SLAB)
            if first:
                h_ref[rows, :] = _rms_norm_f32(x_ref[rows, :], g_ref[...]).astype(BF16)
            h = h_ref[rows, :]
            return _dot(h, wa) * jax.nn.sigmoid(_dot(h, wgate)), _dot(h, wp)

        nslab = bm // ROW_SLAB
        ahead = project(0)
        for q in range(nslab):
            q0 = q * ROW_SLAB
            rows = pl.ds(q0, ROW_SLAB)
            exta_ref[pl.ds(HALO + q0, ROW_SLAB), :] = ahead[0]
            extp_ref[pl.ds(HALO + q0, ROW_SLAB), :] = ahead[1]
            if q + 1 < nslab:
                ahead = project(q + 1)

            for c in range(bn // LANES):
                cs = slice(c * LANES, (c + 1) * LANES)
                for r0 in range(q0, q0 + ROW_SLAB, rb):
                    y_ref[j, pl.ds(r0, rb), cs] = _conv_block(exta_ref, cw_ref, cb_ref, r0, cs)
                    tok, sums = _pool_sums(extp_ref, r0, cs)
                    s = sums[0]
                    for k in range(1, len(sums)):
                        s = jnp.where(j == k, sums[k], s)
                    pos = t_in_seq * bm + r0 + lax.broadcasted_iota(jnp.int32, (rb, 1), 0)
                    cnt = jnp.minimum(pos + 1, window).astype(F32)
                    pooled_ref[pl.ds(r0, rb), cs] = (s / cnt - tok).astype(BF16)
            pb = _dot(pooled_ref[rows, :], pw) * ps_ref[...]
            pb_ref[rows, :] = pb.astype(BF16)

            if last:
                width = nb * bn
                total = y_ref[0, rows, :].sum(axis=-1, keepdims=True)
                for c in range(1, nb):
                    total = total + y_ref[c, rows, :].sum(axis=-1, keepdims=True)
                mu = total / width
                sq = None
                for c in range(nb):
                    yc = y_ref[c, rows, :] - mu
                    part = (yc * yc).sum(axis=-1, keepdims=True)
                    sq = part if sq is None else sq + part
                scale = lax.rsqrt(sq / width + EPS)
                for c in range(nb):
                    cs = slice(c * bn, (c + 1) * bn)
                    yn = (y_ref[c, rows, :] - mu) * scale * lg_ref[:, cs] + lb_ref[:, cs]
                    ao_ref[rows, cs] = (yn * jax.nn.sigmoid(yn)).astype(BF16)
        halo_a_ref[j] = exta_ref[bm:bm + HALO, :]
        halo_p_ref[j] = extp_ref[bm:bm + HALO, :]

    pl.when(j == 0)(lambda: body(True, False))
    pl.when((j > 0) & (j < nb - 1))(lambda: body(False, False))
    pl.when(j == nb - 1)(lambda: body(False, True))


def _ab_out_kernel(ao_ref, pb_ref, x_ref, wo_ref, o_ref, woc_ref):
    i = pl.program_id(0)
    n = pl.program_id(1)
    dc = ao_ref.shape[1]

    @pl.when(i == 0)
    def _():
        woc_ref[n] = wo_ref[...].astype(BF16)

    o_ref[...] = (x_ref[...] + _dot(ao_ref[...], woc_ref[n, 0:dc, :])
                  + _dot(pb_ref[...], woc_ref[n, dc:2 * dc, :]))


def _ab_in(x2d, layer, norm_g, li, w_in, conv_w, conv_b, ln_g, ln_b, pool_w, pool_scale, *,
           seq, bm=1024):
    m, d = x2d.shape
    dc = w_in.shape[2] // 3
    bn = dc // len(POOL_WINDOWS)
    nb = dc // bn
    assert seq % bm == 0 and bm % ROW_SLAB == 0 and ROW_SLAB % ROW_BLOCK == 0
    assert bn % LANES == 0 and nb >= 3
    assert HALO >= CONV_WIDTH - 1 and HALO >= max(POOL_WINDOWS)
    assert POOL_WINDOWS == tuple(2 << k for k in range(nb))
    kern = functools.partial(_ab_in_kernel, bm=bm, tiles_per_seq=seq // bm)
    col = lambda first: (lambda i, j: (li, 0, first + j))
    return pl.pallas_call(
        kern,
        grid=(m // bm, nb),
        in_specs=[
            pl.BlockSpec((bm, d), lambda i, j: (i, 0)),
            _row_spec(layer, d),
            pl.BlockSpec((None, d, bn), col(0)),
            pl.BlockSpec((None, d, bn), col(nb)),
            pl.BlockSpec((None, d, bn), col(2 * nb)),
            pl.BlockSpec((None, CONV_WIDTH, bn), col(0)),
            pl.BlockSpec((None, 1, bn), col(0)),
            _row_spec(li, dc),
            _row_spec(li, dc),
            pl.BlockSpec((None, None, bn, bn), lambda i, j: (li, j, 0, 0)),
            pl.BlockSpec((None, 1, bn), col(0)),
        ],
        out_specs=[pl.BlockSpec((bm, dc), lambda i, j: (i, 0)),
                   pl.BlockSpec((bm, bn), lambda i, j: (i, j))],
        out_shape=[jax.ShapeDtypeStruct((m, dc), BF16),
                   jax.ShapeDtypeStruct((m, dc), BF16)],
        scratch_shapes=[pltpu.VMEM((bm, d), BF16),
                        pltpu.VMEM((bm + HALO, bn), F32),
                        pltpu.VMEM((bm + HALO, bn), F32),
                        pltpu.VMEM((nb, HALO, bn), F32),
                        pltpu.VMEM((nb, HALO, bn), F32),
                        pltpu.VMEM((nb, bm, bn), F32),
                        pltpu.VMEM((bm, bn), BF16)],
        compiler_params=_cparams(("arbitrary", "arbitrary")),
        name="ab_in",
    )(x2d, _rows3(norm_g), w_in, w_in, w_in, conv_w, _rows3(conv_b), _rows3(ln_g),
      _rows3(ln_b), pool_w, _rows3(pool_scale))


def _ab_out(x2d, ao, pb, li, w_out, *, bm=1024, bn=1024):
    m, d = x2d.shape
    dc = ao.shape[1]
    return pl.pallas_call(
        _ab_out_kernel,
        grid=(m // bm, d // bn),
        in_specs=[
            pl.BlockSpec((bm, dc), lambda i, n: (i, 0)),
            pl.BlockSpec((bm, dc), lambda i, n: (i, 0)),
            pl.BlockSpec((bm, bn), lambda i, n: (i, n)),
            pl.BlockSpec((None, 2 * dc, bn), _cached_col_map(li, 0, d // bn)),
        ],
        out_specs=pl.BlockSpec((bm, bn), lambda i, n: (i, n)),
        out_shape=jax.ShapeDtypeStruct((m, d), F32),
        scratch_shapes=[pltpu.VMEM((d // bn, 2 * dc, bn), BF16)],
        compiler_params=_cparams(("arbitrary", "arbitrary")),
        name="ab_out",
    )(ao, pb, x2d, w_out)


def _sgu_in_kernel(x_ref, g_ref, w_ref, lg_ref, lb_ref, u_ref, vn_ref,
                   h_ref, v_ref, mu_ref, sc_ref):
    j = pl.program_id(1)
    nh, _, bn = v_ref.shape

    def z():
        return jax.nn.gelu(_dot(h_ref[...], w_ref[...].astype(BF16)), approximate=True)

    def v_step(first):
        if first:
            h_ref[...] = _rms_norm_f32(x_ref[...], g_ref[...]).astype(BF16)
        v_ref[j] = z()

    pl.when(j == 0)(lambda: v_step(True))
    pl.when((j > 0) & (j < nh))(lambda: v_step(False))

    def u_step(c):
        u_ref[...] = z().astype(BF16)
        if c == 0:
            width = nh * bn
            total = v_ref[0].sum(axis=-1, keepdims=True)
            for k in range(1, nh):
                total = total + v_ref[k].sum(axis=-1, keepdims=True)
            mu = total / width
            sq = None
            for k in range(nh):
                xc = v_ref[k] - mu
                part = (xc * xc).sum(axis=-1, keepdims=True)
                sq = part if sq is None else sq + part
            mu_ref[...] = mu
            sc_ref[...] = lax.rsqrt(sq / width + EPS)
        cs = slice(c * bn, (c + 1) * bn)
        vn = (v_ref[c] - mu_ref[...]) * sc_ref[...] * lg_ref[:, cs] + lb_ref[:, cs]
        vn_ref[:, cs] = vn.astype(BF16)

    for c in range(nh):
        pl.when(j == nh + c)(functools.partial(u_step, c))


def _sgu_in(x2d, layer, norm_g, li, w_in, ln_g, ln_b, *, bm=1024, bn=512):
    m, d = x2d.shape
    ds = w_in.shape[2] // 2
    nh = ds // bn
    return pl.pallas_call(
        _sgu_in_kernel,
        grid=(m // bm, 2 * nh),
        in_specs=[
            pl.BlockSpec((bm, d), lambda i, j: (i, 0)),
            _row_spec(layer, d),
            pl.BlockSpec((None, d, bn), lambda i, j: (li, 0, (j + nh) % (2 * nh))),
            _row_spec(li, ds),
            _row_spec(li, ds),
        ],
        out_specs=[pl.BlockSpec((bm, bn), lambda i, j: (i, jnp.maximum(j - nh, 0))),
                   pl.BlockSpec((bm, ds), lambda i, j: (i, 0))],
        out_shape=[jax.ShapeDtypeStruct((m, ds), BF16),
                   jax.ShapeDtypeStruct((m, ds), BF16)],
        scratch_shapes=[pltpu.VMEM((bm, d), BF16),
                        pltpu.VMEM((nh, bm, bn), F32),
                        pltpu.VMEM((bm, 1), F32),
                        pltpu.VMEM((bm, 1), F32)],
        compiler_params=_cparams(("parallel", "arbitrary")),
        name="sgu_in",
    )(x2d, _rows3(norm_g), w_in, _rows3(ln_g), _rows3(ln_b))


def _sgu_out_kernel(u_ref, vn_ref, sw_ref, sbt_ref, x_ref, wo_ref, o_ref, y_ref, woc_ref,
                    *, bm):
    n = pl.program_id(1)
    ds = vn_ref.shape[1]
    hd = ds // SGU_HEADS

    @pl.when(pl.program_id(1) == 0)
    def _():
        ri = lax.broadcasted_iota(jnp.int32, (SGU_LEN, SGU_LEN), 0) // CHUNK
        ci = lax.broadcasted_iota(jnp.int32, (SGU_LEN, SGU_LEN), 1) // CHUNK
        mask = ci <= ri
        for g in range(SGU_HEADS):
            wg = jnp.where(mask, sw_ref[g], 0.0).astype(BF16)
            bias = sbt_ref[:, g:g + 1]
            gs = slice(g * hd, (g + 1) * hd)
            for w in range(bm // SGU_LEN):
                rs = pl.ds(w * SGU_LEN, SGU_LEN)
                mixed = _dot(wg, vn_ref[rs, gs]) + bias
                y_ref[rs, gs] = (u_ref[rs, gs].astype(F32) * mixed).astype(BF16)

    @pl.when(pl.program_id(0) == 0)
    def _():
        woc_ref[n] = wo_ref[...].astype(BF16)

    o_ref[...] = x_ref[...] + _dot(y_ref[...], woc_ref[n])


def _sgu_out(x2d, u, vn, li, sgu_w, sgu_b, w_out, *, bm=1024, bn=512):
    m, d = x2d.shape
    ds = vn.shape[1]
    kern = functools.partial(_sgu_out_kernel, bm=bm)
    return pl.pallas_call(
        kern,
        grid=(m // bm, d // bn),
        in_specs=[
            pl.BlockSpec((bm, ds), lambda i, n: (i, 0)),
            pl.BlockSpec((bm, ds), lambda i, n: (i, 0)),
            pl.BlockSpec((None,) + sgu_w.shape[1:], lambda i, n: (li, 0, 0, 0)),
            pl.BlockSpec((None, SGU_LEN, SGU_HEADS), lambda i, n: (li, 0, 0)),
            pl.BlockSpec((bm, bn), lambda i, n: (i, n)),
            pl.BlockSpec((None, ds, bn), _cached_col_map(li, 0, d // bn)),
        ],
        out_specs=pl.BlockSpec((bm, bn), lambda i, n: (i, n)),
        out_shape=jax.ShapeDtypeStruct((m, d), F32),
        scratch_shapes=[pltpu.VMEM((bm, ds), BF16),
                        pltpu.VMEM((d // bn, ds, bn), BF16)],
        compiler_params=_cparams(("arbitrary", "arbitrary")),
        name="sgu_out",
    )(u, vn, sgu_w, jnp.swapaxes(sgu_b, 1, 2), x2d, w_out)


def kernel(x, norm_ffn1, ffn1_w_gate, ffn1_w_up, ffn1_w_down, norm_mix, norm_ffn2,
           ffn2_w_gate, ffn2_w_up, ffn2_w_down, ab_w_in, conv_w, conv_b, conv_ln_g,
           conv_ln_b, pool_w, pool_scale, ab_w_out, c_w_in, sgu_ln_g, sgu_ln_b, sgu_w,
           sgu_b, c_w_out, final_norm):
    bsz, seq, d = x.shape
    depth = norm_ffn1.shape[0]
    x2d = x.reshape(bsz * seq, d)
    for layer in range(depth):
        x2d = _ffn(x2d, layer, norm_ffn1, ffn1_w_gate, ffn1_w_up, ffn1_w_down,
                   final_norm, final_norm=False)
        i = layer // 2
        if layer % 2 == 0:
            ao, pb = _ab_in(x2d, layer, norm_mix, i, ab_w_in, conv_w, conv_b, conv_ln_g,
                            conv_ln_b, pool_w, pool_scale, seq=seq)
            x2d = _ab_out(x2d, ao, pb, i, ab_w_out)
        else:
            u, vn = _sgu_in(x2d, layer, norm_mix, i, c_w_in, sgu_ln_g, sgu_ln_b)
            x2d = _sgu_out(x2d, u, vn, i, sgu_w, sgu_b, c_w_out)
        x2d = _ffn(x2d, layer, norm_ffn2, ffn2_w_gate, ffn2_w_up, ffn2_w_down,
                   final_norm, final_norm=(layer == depth - 1))
    return x2d.reshape(bsz, seq, d)
```

```python
import functools

import jax
import jax.numpy as jnp
from jax import lax
from jax.experimental import pallas as pl
from jax.experimental.pallas import tpu as pltpu

EPS = 1e-6
CHUNK = 64
CONV_WIDTH = 31
POOL_WINDOWS = (2, 4, 8, 16)
SGU_LEN = 128
SGU_HEADS = 8
HALO = 32
SUBLANES = 8
LANES = 128
ROW_BLOCK = 32
ROW_SLAB = 256
BF16 = jnp.bfloat16
F32 = jnp.float32

VMEM_LIMIT_BYTES = 58 * 1024 * 1024


def _cparams(semantics):
    return pltpu.CompilerParams(dimension_semantics=semantics,
                                vmem_limit_bytes=VMEM_LIMIT_BYTES)


def _rms_norm_f32(xf, g):
    ms = jnp.mean(xf * xf, axis=-1, keepdims=True)
    return xf * lax.rsqrt(ms + EPS) * g


def _layer_norm_f32(xf, g, b):
    mu = jnp.mean(xf, axis=-1, keepdims=True)
    xc = xf - mu
    var = jnp.mean(xc * xc, axis=-1, keepdims=True)
    return xc * lax.rsqrt(var + EPS) * g + b


def _dot(a, b):
    return jnp.dot(a, b, preferred_element_type=F32)


def _rows3(a):
    return a.reshape(a.shape[0], 1, a.shape[1])


def _row_spec(layer, c):
    return pl.BlockSpec((None, 1, c), lambda *_: (layer, 0, 0))


def _x_tile_prefetch(x_hbm, xbuf, sem):
    i = pl.program_id(0)
    j = pl.program_id(1)
    bm = xbuf.shape[0]

    def x_copy(tile):
        return pltpu.make_async_copy(x_hbm.at[pl.ds(tile * bm, bm), :], xbuf, sem)

    @pl.when((i == 0) & (j == 0))
    def _():
        x_copy(0).start()

    @pl.when((j == 1) & (i + 1 < pl.num_programs(0)))
    def _():
        x_copy(i + 1).start()

    return x_copy(i)


def _ffn_kernel(x_hbm, g_ref, wg_ref, wu_ref, wd_ref, fg_ref, o_ref, h_ref, xbuf, sem, *,
                f_total, bf, final_norm):
    j = pl.program_id(1)
    nj = pl.num_programs(1)
    x_arrival = _x_tile_prefetch(x_hbm, xbuf, sem)

    def step(width, first=False, last=False):
        if first:
            x_arrival.wait()
            h_ref[...] = _rms_norm_f32(xbuf[...], g_ref[...]).astype(BF16)
        h = h_ref[...]

        def swiglu(g, u):
            return (g * jax.nn.sigmoid(g) * u * 0.5).astype(BF16)

        g = _dot(h, wg_ref[:, :width].astype(BF16))
        u = _dot(h, wu_ref[:, :width].astype(BF16))
        down = _dot(swiglu(g, u), wd_ref[:width, :].astype(BF16))
        if first:
            o_ref[...] = xbuf[...] + down
        else:
            o_ref[...] += down
        if last and final_norm:
            o_ref[...] = _rms_norm_f32(o_ref[...], fg_ref[...])

    rem = f_total % bf
    pl.when(j == 0)(lambda: step(bf, first=True))
    pl.when((j > 0) & (j < nj - 1))(lambda: step(bf))
    pl.when(j == nj - 1)(lambda: step(rem if rem else bf, last=True))


def _ffn(x2d, layer, norm_g, wg, wu, wd, final_g, *, final_norm, bm=1024, bf=512):
    m, d = x2d.shape
    f = wg.shape[2]
    nj = pl.cdiv(f, bf)
    assert nj >= 3 and m % bm == 0
    kern = functools.partial(_ffn_kernel, f_total=f, bf=bf, final_norm=final_norm)
    return pl.pallas_call(
        kern,
        grid=(m // bm, nj),
        in_specs=[
            pl.BlockSpec(memory_space=pl.ANY),
            _row_spec(layer, d),
            pl.BlockSpec((None, d, bf), lambda i, j: (layer, 0, j)),
            pl.BlockSpec((None, d, bf), lambda i, j: (layer, 0, j)),
            pl.BlockSpec((None, bf, d), lambda i, j: (layer, j, 0)),
            pl.BlockSpec((1, d), lambda i, j: (0, 0)),
        ],
        out_specs=pl.BlockSpec((bm, d), lambda i, j: (i, 0)),
        out_shape=jax.ShapeDtypeStruct((m, d), F32),
        scratch_shapes=[pltpu.VMEM((bm, d), BF16),
                        pltpu.VMEM((bm, d), F32),
                        pltpu.SemaphoreType.DMA(())],
        compiler_params=_cparams(("arbitrary", "arbitrary")),
        name="ffn_final" if final_norm else "ffn",
    )(x2d, _rows3(norm_g), wg, wu, wd, final_g.reshape(1, d))


def _cached_col_map(layer, first_block, nb):
    return lambda i, j: (layer, 0, first_block + jnp.where(i == 0, j, nb - 1))


def _conv_block(ext_ref, cw_ref, cb_ref, r0, cs):
    rb = ROW_BLOCK
    base = HALO - (CONV_WIDTH - 1)
    e = ext_ref[pl.ds(r0, rb + HALO), cs]
    acc = jnp.broadcast_to(cb_ref[:, cs], (rb, LANES))
    for r in range(SUBLANES):
        qs = [q for q in range((base + CONV_WIDTH - 1) // SUBLANES + 1)
              if 0 <= SUBLANES * q + r - base < CONV_WIDTH]
        s = e if r == 0 else pltpu.roll(e, rb + HALO - r, 0)
        for q in qs:
            k = SUBLANES * q + r - base
            acc = acc + s[SUBLANES * q:SUBLANES * q + rb] * cw_ref[k:k + 1, cs]
    return acc


def _pool_sums(ext_ref, r0, cs):
    rb = ROW_BLOCK
    x = ext_ref[pl.ds(r0, rb + HALO), cs]
    tok = x[HALO:HALO + rb]
    s = x
    start, width = 0, 1
    sums = []
    while width < max(POOL_WINDOWS):
        new_start = start + SUBLANES
        n = rb + HALO - new_start
        off = new_start - start
        s = s[off:off + n] + s[off - width:off - width + n]
        start, width = new_start, 2 * width
        sums.append(s[HALO - start:HALO - start + rb])
    return tok, sums


def _ab_in_kernel(x_hbm, g_ref, wa_ref, wgate_ref, wp_ref, cw_ref, cb_ref, lg_ref, lb_ref,
                  pw_ref, ps_ref, ao_ref, pb_ref,
                  h_ref, exta_ref, extp_ref, halo_a_ref, halo_p_ref, y_ref, pooled_ref,
                  wc_ref, xbuf, sem, *, bm, tiles_per_seq):
    i = pl.program_id(0)
    j = pl.program_id(1)
    nb, _, bn = y_ref.shape
    rb = ROW_BLOCK
    x_arrival = _x_tile_prefetch(x_hbm, xbuf, sem)

    @pl.when(i == 0)
    def _():
        halo_a_ref[j] = jnp.zeros((HALO, bn), F32)
        halo_p_ref[j] = jnp.zeros((HALO, bn), F32)
        wc_ref[0, j] = wa_ref[...].astype(BF16)
        wc_ref[1, j] = wgate_ref[...].astype(BF16)
        wc_ref[2, j] = wp_ref[...].astype(BF16)

    t_in_seq = i % tiles_per_seq
    seq_start = t_in_seq == 0
    window = jnp.left_shift(2, j)

    def body(first, last):
        pw = pw_ref[...].astype(BF16)
        exta_ref[0:HALO, :] = jnp.where(seq_start, 0.0, halo_a_ref[j])
        extp_ref[0:HALO, :] = jnp.where(seq_start, 0.0, halo_p_ref[j])
        if first:
            x_arrival.wait()

        def project(q):
            rows = pl.ds(q * ROW_SLAB, ROW_SLAB)
            if first:
                h_ref[rows, :] = _rms_norm_f32(xbuf[rows, :], g_ref[...]).astype(BF16)
            h = h_ref[rows, :]
            gate = _dot(h, wc_ref[1, j])
            return _dot(h, wc_ref[0, j]) * jax.nn.sigmoid(gate), _dot(h, wc_ref[2, j])

        nslab = bm // ROW_SLAB
        ahead = project(0)
        for q in range(nslab):
            q0 = q * ROW_SLAB
            rows = pl.ds(q0, ROW_SLAB)
            exta_ref[pl.ds(HALO + q0, ROW_SLAB), :] = ahead[0]
            extp_ref[pl.ds(HALO + q0, ROW_SLAB), :] = ahead[1]
            if q + 1 < nslab:
                ahead = project(q + 1)

            for c in range(bn // LANES):
                cs = slice(c * LANES, (c + 1) * LANES)
                for r0 in range(q0, q0 + ROW_SLAB, rb):
                    y_ref[j, pl.ds(r0, rb), cs] = _conv_block(exta_ref, cw_ref, cb_ref, r0, cs)
                    tok, sums = _pool_sums(extp_ref, r0, cs)
                    s = sums[0]
                    for k in range(1, len(sums)):
                        s = jnp.where(j == k, sums[k], s)
                    pos = t_in_seq * bm + r0 + lax.broadcasted_iota(jnp.int32, (rb, 1), 0)
                    cnt = jnp.minimum(pos + 1, window).astype(F32)
                    pooled_ref[pl.ds(r0, rb), cs] = (s / cnt - tok).astype(BF16)
            pb = _dot(pooled_ref[rows, :], pw) * ps_ref[...]
            pb_ref[rows, :] = pb.astype(BF16)

            if last:
                width = nb * bn
                total = y_ref[0, rows, :].sum(axis=-1, keepdims=True)
                for c in range(1, nb):
                    total = total + y_ref[c, rows, :].sum(axis=-1, keepdims=True)
                mu = total / width
                sq = None
                for c in range(nb):
                    yc = y_ref[c, rows, :] - mu
                    part = (yc * yc).sum(axis=-1, keepdims=True)
                    sq = part if sq is None else sq + part
                scale = lax.rsqrt(sq / width + EPS)
                for c in range(nb):
                    cs = slice(c * bn, (c + 1) * bn)
                    yn = (y_ref[c, rows, :] - mu) * scale * lg_ref[:, cs] + lb_ref[:, cs]
                    ao_ref[rows, cs] = (yn * jax.nn.sigmoid(yn)).astype(BF16)
        halo_a_ref[j] = exta_ref[bm:bm + HALO, :]
        halo_p_ref[j] = extp_ref[bm:bm + HALO, :]

    pl.when(j == 0)(lambda: body(True, False))
    pl.when((j > 0) & (j < nb - 1))(lambda: body(False, False))
    pl.when(j == nb - 1)(lambda: body(False, True))


def _ab_out_kernel(ao_ref, pb_ref, x_ref, wo_ref, o_ref, woc_ref):
    i = pl.program_id(0)
    n = pl.program_id(1)
    dc = ao_ref.shape[1]

    @pl.when(i == 0)
    def _():
        woc_ref[n] = wo_ref[...].astype(BF16)

    o_ref[...] = (x_ref[...] + _dot(ao_ref[...], woc_ref[n, 0:dc, :])
                  + _dot(pb_ref[...], woc_ref[n, dc:2 * dc, :]))


def _ab_in(x2d, layer, norm_g, li, w_in, conv_w, conv_b, ln_g, ln_b, pool_w, pool_scale, *,
           seq, bm=1024):
    m, d = x2d.shape
    dc = w_in.shape[2] // 3
    bn = dc // len(POOL_WINDOWS)
    nb = dc // bn
    assert seq % bm == 0 and bm % ROW_SLAB == 0 and ROW_SLAB % ROW_BLOCK == 0
    assert bn % LANES == 0 and nb >= 3
    assert HALO >= CONV_WIDTH - 1 and HALO >= max(POOL_WINDOWS)
    assert POOL_WINDOWS == tuple(2 << k for k in range(nb))
    kern = functools.partial(_ab_in_kernel, bm=bm, tiles_per_seq=seq // bm)
    col = lambda first: (lambda i, j: (li, 0, first + j))
    return pl.pallas_call(
        kern,
        grid=(m // bm, nb),
        in_specs=[
            pl.BlockSpec(memory_space=pl.ANY),
            _row_spec(layer, d),
            pl.BlockSpec((None, d, bn), _cached_col_map(li, 0, nb)),
            pl.BlockSpec((None, d, bn), _cached_col_map(li, nb, nb)),
            pl.BlockSpec((None, d, bn), _cached_col_map(li, 2 * nb, nb)),
            pl.BlockSpec((None, CONV_WIDTH, bn), col(0)),
            pl.BlockSpec((None, 1, bn), col(0)),
            _row_spec(li, dc),
            _row_spec(li, dc),
            pl.BlockSpec((None, None, bn, bn), lambda i, j: (li, j, 0, 0)),
            pl.BlockSpec((None, 1, bn), col(0)),
        ],
        out_specs=[pl.BlockSpec((bm, dc), lambda i, j: (i, 0)),
                   pl.BlockSpec((bm, bn), lambda i, j: (i, j))],
        out_shape=[jax.ShapeDtypeStruct((m, dc), BF16),
                   jax.ShapeDtypeStruct((m, dc), BF16)],
        scratch_shapes=[pltpu.VMEM((bm, d), BF16),
                        pltpu.VMEM((bm + HALO, bn), F32),
                        pltpu.VMEM((bm + HALO, bn), F32),
                        pltpu.VMEM((nb, HALO, bn), F32),
                        pltpu.VMEM((nb, HALO, bn), F32),
                        pltpu.VMEM((nb, bm, bn), F32),
                        pltpu.VMEM((bm, bn), BF16),
                        pltpu.VMEM((3, nb, d, bn), BF16),
                        pltpu.VMEM((bm, d), F32),
                        pltpu.SemaphoreType.DMA(())],
        compiler_params=_cparams(("arbitrary", "arbitrary")),
        name="ab_in",
    )(x2d, _rows3(norm_g), w_in, w_in, w_in, conv_w, _rows3(conv_b), _rows3(ln_g),
      _rows3(ln_b), pool_w, _rows3(pool_scale))


def _ab_out(x2d, ao, pb, li, w_out, *, bm=1024, bn=1024):
    m, d = x2d.shape
    dc = ao.shape[1]
    return pl.pallas_call(
        _ab_out_kernel,
        grid=(m // bm, d // bn),
        in_specs=[
            pl.BlockSpec((bm, dc), lambda i, n: (i, 0)),
            pl.BlockSpec((bm, dc), lambda i, n: (i, 0)),
            pl.BlockSpec((bm, bn), lambda i, n: (i, n)),
            pl.BlockSpec((None, 2 * dc, bn), _cached_col_map(li, 0, d // bn)),
        ],
        out_specs=pl.BlockSpec((bm, bn), lambda i, n: (i, n)),
        out_shape=jax.ShapeDtypeStruct((m, d), F32),
        scratch_shapes=[pltpu.VMEM((d // bn, 2 * dc, bn), BF16)],
        compiler_params=_cparams(("arbitrary", "arbitrary")),
        name="ab_out",
    )(ao, pb, x2d, w_out)


def _sgu_in_kernel(x_hbm, g_ref, w_ref, lg_ref, lb_ref, u_ref, vn_ref,
                   h_ref, v_ref, mu_ref, sc_ref, xbuf, sem):
    j = pl.program_id(1)
    nh, _, bn = v_ref.shape
    x_arrival = _x_tile_prefetch(x_hbm, xbuf, sem)

    def z():
        return jax.nn.gelu(_dot(h_ref[...], w_ref[...].astype(BF16)), approximate=True)

    def v_step(first):
        if first:
            x_arrival.wait()
            h_ref[...] = _rms_norm_f32(xbuf[...], g_ref[...]).astype(BF16)
        v_ref[j] = z()

    pl.when(j == 0)(lambda: v_step(True))
    pl.when((j > 0) & (j < nh))(lambda: v_step(False))

    def u_step(c):
        u_ref[...] = z().astype(BF16)
        if c == 0:
            width = nh * bn
            total = v_ref[0].sum(axis=-1, keepdims=True)
            for k in range(1, nh):
                total = total + v_ref[k].sum(axis=-1, keepdims=True)
            mu = total / width
            sq = None
            for k in range(nh):
                xc = v_ref[k] - mu
                part = (xc * xc).sum(axis=-1, keepdims=True)
                sq = part if sq is None else sq + part
            mu_ref[...] = mu
            sc_ref[...] = lax.rsqrt(sq / width + EPS)
        cs = slice(c * bn, (c + 1) * bn)
        vn = (v_ref[c] - mu_ref[...]) * sc_ref[...] * lg_ref[:, cs] + lb_ref[:, cs]
        vn_ref[:, cs] = vn.astype(BF16)

    for c in range(nh):
        pl.when(j == nh + c)(functools.partial(u_step, c))


def _sgu_in(x2d, layer, norm_g, li, w_in, ln_g, ln_b, *, bm=1024, bn=1024):
    m, d = x2d.shape
    ds = w_in.shape[2] // 2
    nh = ds // bn
    assert nh >= 2 and m % bm == 0
    return pl.pallas_call(
        _sgu_in_kernel,
        grid=(m // bm, 2 * nh),
        in_specs=[
            pl.BlockSpec(memory_space=pl.ANY),
            _row_spec(layer, d),
            pl.BlockSpec((None, d, bn), lambda i, j: (li, 0, (j + nh) % (2 * nh))),
            _row_spec(li, ds),
            _row_spec(li, ds),
        ],
        out_specs=[pl.BlockSpec((bm, bn), lambda i, j: (i, jnp.maximum(j - nh, 0))),
                   pl.BlockSpec((bm, ds), lambda i, j: (i, 0))],
        out_shape=[jax.ShapeDtypeStruct((m, ds), BF16),
                   jax.ShapeDtypeStruct((m, ds), BF16)],
        scratch_shapes=[pltpu.VMEM((bm, d), BF16),
                        pltpu.VMEM((nh, bm, bn), F32),
                        pltpu.VMEM((bm, 1), F32),
                        pltpu.VMEM((bm, 1), F32),
                        pltpu.VMEM((bm, d), F32),
                        pltpu.SemaphoreType.DMA(())],
        compiler_params=_cparams(("arbitrary", "arbitrary")),
        name="sgu_in",
    )(x2d, _rows3(norm_g), w_in, _rows3(ln_g), _rows3(ln_b))


def _sgu_out_kernel(u_ref, vn_ref, sw_ref, sbt_ref, x_ref, wo_ref, o_ref, y_ref, woc_ref,
                    *, bm):
    n = pl.program_id(1)
    ds = vn_ref.shape[1]
    hd = ds // SGU_HEADS

    @pl.when(pl.program_id(1) == 0)
    def _():
        ri = lax.broadcasted_iota(jnp.int32, (SGU_LEN, SGU_LEN), 0) // CHUNK
        ci = lax.broadcasted_iota(jnp.int32, (SGU_LEN, SGU_LEN), 1) // CHUNK
        mask = ci <= ri
        for g in range(SGU_HEADS):
            wg = jnp.where(mask, sw_ref[g], 0.0).astype(BF16)
            bias = sbt_ref[:, g:g + 1]
            gs = slice(g * hd, (g + 1) * hd)
            for w in range(bm // SGU_LEN):
                rs = pl.ds(w * SGU_LEN, SGU_LEN)
                mixed = _dot(wg, vn_ref[rs, gs]) + bias
                y_ref[rs, gs] = (u_ref[rs, gs].astype(F32) * mixed).astype(BF16)

    @pl.when(pl.program_id(0) == 0)
    def _():
        woc_ref[n] = wo_ref[...].astype(BF16)

    o_ref[...] = x_ref[...] + _dot(y_ref[...], woc_ref[n])


def _sgu_out(x2d, u, vn, li, sgu_w, sgu_b, w_out, *, bm=1024, bn=512):
    m, d = x2d.shape
    ds = vn.shape[1]
    kern = functools.partial(_sgu_out_kernel, bm=bm)
    return pl.pallas_call(
        kern,
        grid=(m // bm, d // bn),
        in_specs=[
            pl.BlockSpec((bm, ds), lambda i, n: (i, 0)),
            pl.BlockSpec((bm, ds), lambda i, n: (i, 0)),
            pl.BlockSpec((None,) + sgu_w.shape[1:], lambda i, n: (li, 0, 0, 0)),
            pl.BlockSpec((None, SGU_LEN, SGU_HEADS), lambda i, n: (li, 0, 0)),
            pl.BlockSpec((bm, bn), lambda i, n: (i, n)),
            pl.BlockSpec((None, ds, bn), _cached_col_map(li, 0, d // bn)),
        ],
        out_specs=pl.BlockSpec((bm, bn), lambda i, n: (i, n)),
        out_shape=jax.ShapeDtypeStruct((m, d), F32),
        scratch_shapes=[pltpu.VMEM((bm, ds), BF16),
                        pltpu.VMEM((d // bn, ds, bn), BF16)],
        compiler_params=_cparams(("arbitrary", "arbitrary")),
        name="sgu_out",
    )(u, vn, sgu_w, jnp.swapaxes(sgu_b, 1, 2), x2d, w_out)


def kernel(x, norm_ffn1, ffn1_w_gate, ffn1_w_up, ffn1_w_down, norm_mix, norm_ffn2,
           ffn2_w_gate, ffn2_w_up, ffn2_w_down, ab_w_in, conv_w, conv_b, conv_ln_g,
           conv_ln_b, pool_w, pool_scale, ab_w_out, c_w_in, sgu_ln_g, sgu_ln_b, sgu_w,
           sgu_b, c_w_out, final_norm):
    bsz, seq, d = x.shape
    depth = norm_ffn1.shape[0]
    x2d = x.reshape(bsz * seq, d)
    for layer in range(depth):
        x2d = _ffn(x2d, layer, norm_ffn1, ffn1_w_gate, ffn1_w_up, ffn1_w_down,
                   final_norm, final_norm=False)
        i = layer // 2
        if layer % 2 == 0:
            ao, pb = _ab_in(x2d, layer, norm_mix, i, ab_w_in, conv_w, conv_b, conv_ln_g,
                            conv_ln_b, pool_w, pool_scale, seq=seq)
            x2d = _ab_out(x2d, ao, pb, i, ab_w_out)
        else:
            u, vn = _sgu_in(x2d, layer, norm_mix, i, c_w_in, sgu_ln_g, sgu_ln_b)
            x2d = _sgu_out(x2d, u, vn, i, sgu_w, sgu_b, c_w_out)
        x2d = _ffn(x2d, layer, norm_ffn2, ffn2_w_gate, ffn2_w_up, ffn2_w_down,
                   final_norm, final_norm=(layer == depth - 1))
    return x2d.reshape(bsz, seq, d)
```

```python
import functools

import jax
import jax.numpy as jnp
from jax import lax
from jax.experimental import pallas as pl
from jax.experimental.pallas import tpu as pltpu

EPS = 1e-6
CHUNK = 64
CONV_WIDTH = 31
POOL_WINDOWS = (2, 4, 8, 16)
SGU_LEN = 128
SGU_HEADS = 8
HALO = 32
SUBLANES = 8
LANES = 128
ROW_BLOCK = 32
ROW_SLAB = 128
BF16 = jnp.bfloat16
F32 = jnp.float32

VMEM_LIMIT_BYTES = 58 * 1024 * 1024


def _cparams(semantics):
    return pltpu.CompilerParams(dimension_semantics=semantics,
                                vmem_limit_bytes=VMEM_LIMIT_BYTES)


def _rms_norm_f32(xf, g):
    ms = jnp.mean(xf * xf, axis=-1, keepdims=True)
    return xf * lax.rsqrt(ms + EPS) * g


def _layer_norm_f32(xf, g, b):
    mu = jnp.mean(xf, axis=-1, keepdims=True)
    xc = xf - mu
    var = jnp.mean(xc * xc, axis=-1, keepdims=True)
    return xc * lax.rsqrt(var + EPS) * g + b


def _dot(a, b):
    return jnp.dot(a, b, preferred_element_type=F32)


def _rows3(a):
    return a.reshape(a.shape[0], 1, a.shape[1])


def _row_spec(layer, c):
    return pl.BlockSpec((None, 1, c), lambda *_: (layer, 0, 0))


def _x_tile_prefetch(x_hbm, xbuf, sem):
    i = pl.program_id(0)
    j = pl.program_id(1)
    bm = xbuf.shape[0]

    def x_copy(tile):
        return pltpu.make_async_copy(x_hbm.at[pl.ds(tile * bm, bm), :], xbuf, sem)

    @pl.when((i == 0) & (j == 0))
    def _():
        x_copy(0).start()

    @pl.when((j == 1) & (i + 1 < pl.num_programs(0)))
    def _():
        x_copy(i + 1).start()

    return x_copy(i)


def _ffn_kernel(x_hbm, g_ref, wg_ref, wu_ref, wd_ref, fg_ref, o_ref, h_ref, xbuf, sem, *,
                f_total, bf, final_norm):
    j = pl.program_id(1)
    nj = pl.num_programs(1)
    x_arrival = _x_tile_prefetch(x_hbm, xbuf, sem)

    def step(width, first=False, last=False):
        if first:
            x_arrival.wait()
            h_ref[...] = _rms_norm_f32(xbuf[...], g_ref[...]).astype(BF16)
        h = h_ref[...]

        def swiglu(g, u):
            return (g * jax.nn.sigmoid(g) * u * 0.5).astype(BF16)

        g = _dot(h, wg_ref[:, :width].astype(BF16))
        u = _dot(h, wu_ref[:, :width].astype(BF16))
        down = _dot(swiglu(g, u), wd_ref[:width, :].astype(BF16))
        if first:
            o_ref[...] = xbuf[...] + down
        else:
            o_ref[...] += down
        if last and final_norm:
            o_ref[...] = _rms_norm_f32(o_ref[...], fg_ref[...])

    rem = f_total % bf
    pl.when(j == 0)(lambda: step(bf, first=True))
    pl.when((j > 0) & (j < nj - 1))(lambda: step(bf))
    pl.when(j == nj - 1)(lambda: step(rem if rem else bf, last=True))


def _ffn(x2d, layer, norm_g, wg, wu, wd, final_g, *, final_norm, bm=1024, bf=512):
    m, d = x2d.shape
    f = wg.shape[2]
    nj = pl.cdiv(f, bf)
    assert nj >= 3 and m % bm == 0
    kern = functools.partial(_ffn_kernel, f_total=f, bf=bf, final_norm=final_norm)
    return pl.pallas_call(
        kern,
        grid=(m // bm, nj),
        in_specs=[
            pl.BlockSpec(memory_space=pl.ANY),
            _row_spec(layer, d),
            pl.BlockSpec((None, d, bf), lambda i, j: (layer, 0, j)),
            pl.BlockSpec((None, d, bf), lambda i, j: (layer, 0, j)),
            pl.BlockSpec((None, bf, d), lambda i, j: (layer, j, 0)),
            pl.BlockSpec((1, d), lambda i, j: (0, 0)),
        ],
        out_specs=pl.BlockSpec((bm, d), lambda i, j: (i, 0)),
        out_shape=jax.ShapeDtypeStruct((m, d), F32),
        scratch_shapes=[pltpu.VMEM((bm, d), BF16),
                        pltpu.VMEM((bm, d), F32),
                        pltpu.SemaphoreType.DMA(())],
        compiler_params=_cparams(("arbitrary", "arbitrary")),
        name="ffn_final" if final_norm else "ffn",
    )(x2d, _rows3(norm_g), wg, wu, wd, final_g.reshape(1, d))


def _cached_col_map(layer, first_block, nb):
    return lambda i, j: (layer, 0, first_block + jnp.where(i == 0, j, nb - 1))


def _conv_block(ext_ref, cw_ref, cb_ref, r0, cs):
    rb = ROW_BLOCK
    base = HALO - (CONV_WIDTH - 1)
    e = ext_ref[pl.ds(r0, rb + HALO), cs]
    acc = jnp.broadcast_to(cb_ref[:, cs], (rb, LANES))
    for r in range(SUBLANES):
        qs = [q for q in range((base + CONV_WIDTH - 1) // SUBLANES + 1)
              if 0 <= SUBLANES * q + r - base < CONV_WIDTH]
        s = e if r == 0 else pltpu.roll(e, rb + HALO - r, 0)
        for q in qs:
            k = SUBLANES * q + r - base
            acc = acc + s[SUBLANES * q:SUBLANES * q + rb] * cw_ref[k:k + 1, cs]
    return acc


def _pool_sums(ext_ref, r0, cs):
    rb = ROW_BLOCK
    x = ext_ref[pl.ds(r0, rb + HALO), cs]
    tok = x[HALO:HALO + rb]
    s = x
    start, width = 0, 1
    sums = []
    while width < max(POOL_WINDOWS):
        new_start = start + SUBLANES
        n = rb + HALO - new_start
        off = new_start - start
        s = s[off:off + n] + s[off - width:off - width + n]
        start, width = new_start, 2 * width
        sums.append(s[HALO - start:HALO - start + rb])
    return tok, sums


def _ab_in_kernel(x_hbm, g_ref, wa_ref, wgate_ref, wp_ref, cw_ref, cb_ref, lg_ref, lb_ref,
                  pw_ref, ps_ref, ao_ref, pb_ref,
                  h_ref, exta_ref, extp_ref, halo_a_ref, halo_p_ref, y_ref, pooled_ref,
                  wc_ref, xbuf, sem, *, bm, tiles_per_seq):
    i = pl.program_id(0)
    j = pl.program_id(1)
    nb, _, bn = y_ref.shape
    rb = ROW_BLOCK
    x_arrival = _x_tile_prefetch(x_hbm, xbuf, sem)

    @pl.when(i == 0)
    def _():
        halo_a_ref[j] = jnp.zeros((HALO, bn), F32)
        halo_p_ref[j] = jnp.zeros((HALO, bn), F32)
        wc_ref[0, j] = wa_ref[...].astype(BF16)
        wc_ref[1, j] = wgate_ref[...].astype(BF16)
        wc_ref[2, j] = wp_ref[...].astype(BF16)

    t_in_seq = i % tiles_per_seq
    seq_start = t_in_seq == 0
    window = jnp.left_shift(2, j)

    def body(first, last):
        pw = pw_ref[...].astype(BF16)
        exta_ref[0:HALO, :] = jnp.where(seq_start, 0.0, halo_a_ref[j])
        extp_ref[0:HALO, :] = jnp.where(seq_start, 0.0, halo_p_ref[j])
        if first:
            x_arrival.wait()

        def project(q):
            rows = pl.ds(q * ROW_SLAB, ROW_SLAB)
            if first:
                h_ref[rows, :] = _rms_norm_f32(xbuf[rows, :], g_ref[...]).astype(BF16)
            h = h_ref[rows, :]
            gate = _dot(h, wc_ref[1, j])
            return _dot(h, wc_ref[0, j]) * jax.nn.sigmoid(gate), _dot(h, wc_ref[2, j])

        nslab = bm // ROW_SLAB
        ahead = project(0)
        for q in range(nslab):
            q0 = q * ROW_SLAB
            rows = pl.ds(q0, ROW_SLAB)
            exta_ref[pl.ds(HALO + q0, ROW_SLAB), :] = ahead[0]
            extp_ref[pl.ds(HALO + q0, ROW_SLAB), :] = ahead[1]
            if q + 1 < nslab:
                ahead = project(q + 1)

            for c in range(bn // LANES):
                cs = slice(c * LANES, (c + 1) * LANES)
                for r0 in range(q0, q0 + ROW_SLAB, rb):
                    y_ref[j, pl.ds(r0, rb), cs] = _conv_block(exta_ref, cw_ref, cb_ref, r0, cs)
                    tok, sums = _pool_sums(extp_ref, r0, cs)
                    s = sums[0]
                    for k in range(1, len(sums)):
                        s = jnp.where(j == k, sums[k], s)
                    pos = t_in_seq * bm + r0 + lax.broadcasted_iota(jnp.int32, (rb, 1), 0)
                    cnt = jnp.minimum(pos + 1, window).astype(F32)
                    pooled_ref[pl.ds(r0, rb), cs] = (s / cnt - tok).astype(BF16)
            pb = _dot(pooled_ref[rows, :], pw) * ps_ref[...]
            pb_ref[rows, :] = pb.astype(BF16)

            if last:
                width = nb * bn
                total = y_ref[0, rows, :].sum(axis=-1, keepdims=True)
                for c in range(1, nb):
                    total = total + y_ref[c, rows, :].sum(axis=-1, keepdims=True)
                mu = total / width
                sq = None
                for c in range(nb):
                    yc = y_ref[c, rows, :] - mu
                    part = (yc * yc).sum(axis=-1, keepdims=True)
                    sq = part if sq is None else sq + part
                scale = lax.rsqrt(sq / width + EPS)
                for c in range(nb):
                    cs = slice(c * bn, (c + 1) * bn)
                    yn = (y_ref[c, rows, :] - mu) * scale * lg_ref[:, cs] + lb_ref[:, cs]
                    ao_ref[rows, cs] = (yn * jax.nn.sigmoid(yn)).astype(BF16)
        halo_a_ref[j] = exta_ref[bm:bm + HALO, :]
        halo_p_ref[j] = extp_ref[bm:bm + HALO, :]

    pl.when(j == 0)(lambda: body(True, False))
    pl.when((j > 0) & (j < nb - 1))(lambda: body(False, False))
    pl.when(j == nb - 1)(lambda: body(False, True))


def _ab_out_kernel(ao_ref, pb_ref, x_ref, wo_ref, o_ref, woc_ref):
    i = pl.program_id(0)
    n = pl.program_id(1)
    dc = ao_ref.shape[1]

    @pl.when(i == 0)
    def _():
        woc_ref[n] = wo_ref[...].astype(BF16)

    o_ref[...] = (x_ref[...] + _dot(ao_ref[...], woc_ref[n, 0:dc, :])
                  + _dot(pb_ref[...], woc_ref[n, dc:2 * dc, :]))


def _ab_in(x2d, layer, norm_g, li, w_in, conv_w, conv_b, ln_g, ln_b, pool_w, pool_scale, *,
           seq, bm=1024):
    m, d = x2d.shape
    dc = w_in.shape[2] // 3
    bn = dc // len(POOL_WINDOWS)
    nb = dc // bn
    assert seq % bm == 0 and bm % ROW_SLAB == 0 and ROW_SLAB % ROW_BLOCK == 0
    assert bn % LANES == 0 and nb >= 3
    assert HALO >= CONV_WIDTH - 1 and HALO >= max(POOL_WINDOWS)
    assert POOL_WINDOWS == tuple(2 << k for k in range(nb))
    kern = functools.partial(_ab_in_kernel, bm=bm, tiles_per_seq=seq // bm)
    col = lambda first: (lambda i, j: (li, 0, first + j))
    return pl.pallas_call(
        kern,
        grid=(m // bm, nb),
        in_specs=[
            pl.BlockSpec(memory_space=pl.ANY),
            _row_spec(layer, d),
            pl.BlockSpec((None, d, bn), _cached_col_map(li, 0, nb)),
            pl.BlockSpec((None, d, bn), _cached_col_map(li, nb, nb)),
            pl.BlockSpec((None, d, bn), _cached_col_map(li, 2 * nb, nb)),
            pl.BlockSpec((None, CONV_WIDTH, bn), col(0)),
            pl.BlockSpec((None, 1, bn), col(0)),
            _row_spec(li, dc),
            _row_spec(li, dc),
            pl.BlockSpec((None, None, bn, bn), lambda i, j: (li, j, 0, 0)),
            pl.BlockSpec((None, 1, bn), col(0)),
        ],
        out_specs=[pl.BlockSpec((bm, dc), lambda i, j: (i, 0)),
                   pl.BlockSpec((bm, bn), lambda i, j: (i, j))],
        out_shape=[jax.ShapeDtypeStruct((m, dc), BF16),
                   jax.ShapeDtypeStruct((m, dc), BF16)],
        scratch_shapes=[pltpu.VMEM((bm, d), BF16),
                        pltpu.VMEM((bm + HALO, bn), F32),
                        pltpu.VMEM((bm + HALO, bn), F32),
                        pltpu.VMEM((nb, HALO, bn), F32),
                        pltpu.VMEM((nb, HALO, bn), F32),
                        pltpu.VMEM((nb, bm, bn), F32),
                        pltpu.VMEM((bm, bn), BF16),
                        pltpu.VMEM((3, nb, d, bn), BF16),
                        pltpu.VMEM((bm, d), F32),
                        pltpu.SemaphoreType.DMA(())],
        compiler_params=_cparams(("arbitrary", "arbitrary")),
        name="ab_in",
    )(x2d, _rows3(norm_g), w_in, w_in, w_in, conv_w, _rows3(conv_b), _rows3(ln_g),
      _rows3(ln_b), pool_w, _rows3(pool_scale))


def _ab_out(x2d, ao, pb, li, w_out, *, bm=1024, bn=1024):
    m, d = x2d.shape
    dc = ao.shape[1]
    return pl.pallas_call(
        _ab_out_kernel,
        grid=(m // bm, d // bn),
        in_specs=[
            pl.BlockSpec((bm, dc), lambda i, n: (i, 0)),
            pl.BlockSpec((bm, dc), lambda i, n: (i, 0)),
            pl.BlockSpec((bm, bn), lambda i, n: (i, n)),
            pl.BlockSpec((None, 2 * dc, bn), _cached_col_map(li, 0, d // bn)),
        ],
        out_specs=pl.BlockSpec((bm, bn), lambda i, n: (i, n)),
        out_shape=jax.ShapeDtypeStruct((m, d), F32),
        scratch_shapes=[pltpu.VMEM((d // bn, 2 * dc, bn), BF16)],
        compiler_params=_cparams(("arbitrary", "arbitrary")),
        name="ab_out",
    )(ao, pb, x2d, w_out)


def _sgu_in_kernel(x_hbm, g_ref, w_ref, lg_ref, lb_ref, u_ref, vn_ref,
                   h_ref, v_ref, mu_ref, sc_ref, xbuf, sem):
    j = pl.program_id(1)
    nh, _, bn = v_ref.shape
    x_arrival = _x_tile_prefetch(x_hbm, xbuf, sem)

    def z():
        return jax.nn.gelu(_dot(h_ref[...], w_ref[...].astype(BF16)), approximate=True)

    def v_step(first):
        if first:
            x_arrival.wait()
            h_ref[...] = _rms_norm_f32(xbuf[...], g_ref[...]).astype(BF16)
        v_ref[j] = z()

    pl.when(j == 0)(lambda: v_step(True))
    pl.when((j > 0) & (j < nh))(lambda: v_step(False))

    def u_step(c):
        u_ref[...] = z().astype(BF16)
        if c == 0:
            width = nh * bn
            total = v_ref[0].sum(axis=-1, keepdims=True)
            for k in range(1, nh):
                total = total + v_ref[k].sum(axis=-1, keepdims=True)
            mu = total / width
            sq = None
            for k in range(nh):
                xc = v_ref[k] - mu
                part = (xc * xc).sum(axis=-1, keepdims=True)
                sq = part if sq is None else sq + part
            mu_ref[...] = mu
            sc_ref[...] = lax.rsqrt(sq / width + EPS)
        cs = slice(c * bn, (c + 1) * bn)
        vn = (v_ref[c] - mu_ref[...]) * sc_ref[...] * lg_ref[:, cs] + lb_ref[:, cs]
        vn_ref[:, cs] = vn.astype(BF16)

    for c in range(nh):
        pl.when(j == nh + c)(functools.partial(u_step, c))


def _sgu_in(x2d, layer, norm_g, li, w_in, ln_g, ln_b, *, bm=1024, bn=1024):
    m, d = x2d.shape
    ds = w_in.shape[2] // 2
    nh = ds // bn
    assert nh >= 2 and m % bm == 0
    return pl.pallas_call(
        _sgu_in_kernel,
        grid=(m // bm, 2 * nh),
        in_specs=[
            pl.BlockSpec(memory_space=pl.ANY),
            _row_spec(layer, d),
            pl.BlockSpec((None, d, bn), lambda i, j: (li, 0, (j + nh) % (2 * nh))),
            _row_spec(li, ds),
            _row_spec(li, ds),
        ],
        out_specs=[pl.BlockSpec((bm, bn), lambda i, j: (i, jnp.maximum(j - nh, 0))),
                   pl.BlockSpec((bm, ds), lambda i, j: (i, 0))],
        out_shape=[jax.ShapeDtypeStruct((m, ds), BF16),
                   jax.ShapeDtypeStruct((m, ds), BF16)],
        scratch_shapes=[pltpu.VMEM((bm, d), BF16),
                        pltpu.VMEM((nh, bm, bn), F32),
                        pltpu.VMEM((bm, 1), F32),
                        pltpu.VMEM((bm, 1), F32),
                        pltpu.VMEM((bm, d), F32),
                        pltpu.SemaphoreType.DMA(())],
        compiler_params=_cparams(("arbitrary", "arbitrary")),
        name="sgu_in",
    )(x2d, _rows3(norm_g), w_in, _rows3(ln_g), _rows3(ln_b))


def _sgu_out_kernel(u_ref, vn_ref, sw_ref, sbt_ref, x_ref, wo_ref, o_ref, y_ref, woc_ref,
                    *, bm):
    n = pl.program_id(1)
    ds = vn_ref.shape[1]
    hd = ds // SGU_HEADS

    @pl.when(pl.program_id(1) == 0)
    def _():
        ri = lax.broadcasted_iota(jnp.int32, (SGU_LEN, SGU_LEN), 0) // CHUNK
        ci = lax.broadcasted_iota(jnp.int32, (SGU_LEN, SGU_LEN), 1) // CHUNK
        mask = ci <= ri
        for g in range(SGU_HEADS):
            wg = jnp.where(mask, sw_ref[g], 0.0).astype(BF16)
            bias = sbt_ref[:, g:g + 1]
            gs = slice(g * hd, (g + 1) * hd)
            for w in range(bm // SGU_LEN):
                rs = pl.ds(w * SGU_LEN, SGU_LEN)
                mixed = _dot(wg, vn_ref[rs, gs]) + bias
                y_ref[rs, gs] = (u_ref[rs, gs].astype(F32) * mixed).astype(BF16)

    @pl.when(pl.program_id(0) == 0)
    def _():
        woc_ref[n] = wo_ref[...].astype(BF16)

    o_ref[...] = x_ref[...] + _dot(y_ref[...], woc_ref[n])


def _sgu_out(x2d, u, vn, li, sgu_w, sgu_b, w_out, *, bm=1024, bn=512):
    m, d = x2d.shape
    ds = vn.shape[1]
    kern = functools.partial(_sgu_out_kernel, bm=bm)
    return pl.pallas_call(
        kern,
        grid=(m // bm, d // bn),
        in_specs=[
            pl.BlockSpec((bm, ds), lambda i, n: (i, 0)),
            pl.BlockSpec((bm, ds), lambda i, n: (i, 0)),
            pl.BlockSpec((None,) + sgu_w.shape[1:], lambda i, n: (li, 0, 0, 0)),
            pl.BlockSpec((None, SGU_LEN, SGU_HEADS), lambda i, n: (li, 0, 0)),
            pl.BlockSpec((bm, bn), lambda i, n: (i, n)),
            pl.BlockSpec((None, ds, bn), _cached_col_map(li, 0, d // bn)),
        ],
        out_specs=pl.BlockSpec((bm, bn), lambda i, n: (i, n)),
        out_shape=jax.ShapeDtypeStruct((m, d), F32),
        scratch_shapes=[pltpu.VMEM((bm, ds), BF16),
                        pltpu.VMEM((d // bn, ds, bn), BF16)],
        compiler_params=_cparams(("arbitrary", "arbitrary")),
        name="sgu_out",
    )(u, vn, sgu_w, jnp.swapaxes(sgu_b, 1, 2), x2d, w_out)


def kernel(x, norm_ffn1, ffn1_w_gate, ffn1_w_up, ffn1_w_down, norm_mix, norm_ffn2,
           ffn2_w_gate, ffn2_w_up, ffn2_w_down, ab_w_in, conv_w, conv_b, conv_ln_g,
           conv_ln_b, pool_w, pool_scale, ab_w_out, c_w_in, sgu_ln_g, sgu_ln_b, sgu_w,
           sgu_b, c_w_out, final_norm):
    bsz, seq, d = x.shape
    depth = norm_ffn1.shape[0]
    x2d = x.reshape(bsz * seq, d)
    for layer in range(depth):
        x2d = _ffn(x2d, layer, norm_ffn1, ffn1_w_gate, ffn1_w_up, ffn1_w_down,
                   final_norm, final_norm=False)
        i = layer // 2
        if layer % 2 == 0:
            ao, pb = _ab_in(x2d, layer, norm_mix, i, ab_w_in, conv_w, conv_b, conv_ln_g,
                            conv_ln_b, pool_w, pool_scale, seq=seq)
            x2d = _ab_out(x2d, ao, pb, i, ab_w_out)
        else:
            u, vn = _sgu_in(x2d, layer, norm_mix, i, c_w_in, sgu_ln_g, sgu_ln_b)
            x2d = _sgu_out(x2d, u, vn, i, sgu_w, sgu_b, c_w_out)
        x2d = _ffn(x2d, layer, norm_ffn2, ffn2_w_gate, ffn2_w_up, ffn2_w_down,
                   final_norm, final_norm=(layer == depth - 1))
    return x2d.reshape(bsz, seq, d)
```

```python
import functools

import jax
import jax.numpy as jnp
from jax import lax
from jax.experimental import pallas as pl
from jax.experimental.pallas import tpu as pltpu

EPS = 1e-6
CHUNK = 64
CONV_WIDTH = 31
POOL_WINDOWS = (2, 4, 8, 16)
SGU_LEN = 128
SGU_HEADS = 8
HALO = 32
SUBLANES = 8
LANES = 128
ROW_BLOCK = 32
ROW_SLAB = 256
BF16 = jnp.bfloat16
F32 = jnp.float32

VMEM_LIMIT_BYTES = 58 * 1024 * 1024


def _cparams(semantics):
    return pltpu.CompilerParams(dimension_semantics=semantics,
                                vmem_limit_bytes=VMEM_LIMIT_BYTES)


def _rms_norm_f32(xf, g):
    ms = jnp.mean(xf * xf, axis=-1, keepdims=True)
    return xf * lax.rsqrt(ms + EPS) * g


def _layer_norm_stats(load, nchunks, width):
    total = load(0).sum(axis=-1, keepdims=True)
    for c in range(1, nchunks):
        total = total + load(c).sum(axis=-1, keepdims=True)
    mu = total / width
    sq = None
    for c in range(nchunks):
        xc = load(c) - mu
        part = (xc * xc).sum(axis=-1, keepdims=True)
        sq = part if sq is None else sq + part
    return mu, lax.rsqrt(sq / width + EPS)


def _dot(a, b):
    return jnp.dot(a, b, preferred_element_type=F32)


def _rows3(a):
    return a.reshape(a.shape[0], 1, a.shape[1])


def _row_spec(layer, c):
    return pl.BlockSpec((None, 1, c), lambda *_: (layer, 0, 0))


def _x_tile_prefetch(x_hbm, xbuf, sem):
    i = pl.program_id(0)
    j = pl.program_id(1)
    bm = xbuf.shape[0]

    def x_copy(tile):
        return pltpu.make_async_copy(x_hbm.at[pl.ds(tile * bm, bm), :], xbuf, sem)

    @pl.when((i == 0) & (j == 0))
    def _():
        x_copy(0).start()

    @pl.when((j == 1) & (i + 1 < pl.num_programs(0)))
    def _():
        x_copy(i + 1).start()

    return x_copy(i)


def _ffn_kernel(x_hbm, g_ref, wg_ref, wu_ref, wd_ref, fg_ref, o_ref, h_ref, xbuf, sem, *,
                f_total, bf, final_norm):
    j = pl.program_id(1)
    nj = pl.num_programs(1)
    x_arrival = _x_tile_prefetch(x_hbm, xbuf, sem)

    def step(width, first=False, last=False):
        if first:
            x_arrival.wait()
            h_ref[...] = _rms_norm_f32(xbuf[...], g_ref[...]).astype(BF16)
        h = h_ref[...]

        def swiglu(g, u):
            return (g * jax.nn.sigmoid(g) * u * 0.5).astype(BF16)

        g = _dot(h, wg_ref[:, :width].astype(BF16))
        u = _dot(h, wu_ref[:, :width].astype(BF16))
        down = _dot(swiglu(g, u), wd_ref[:width, :].astype(BF16))
        if first:
            o_ref[...] = xbuf[...] + down
        else:
            o_ref[...] += down
        if last and final_norm:
            o_ref[...] = _rms_norm_f32(o_ref[...], fg_ref[...])

    rem = f_total % bf
    pl.when(j == 0)(lambda: step(bf, first=True))
    pl.when((j > 0) & (j < nj - 1))(lambda: step(bf))
    pl.when(j == nj - 1)(lambda: step(rem if rem else bf, last=True))


def _ffn(x2d, layer, norm_g, wg, wu, wd, final_g, *, final_norm, bm=1024, bf=512):
    m, d = x2d.shape
    f = wg.shape[2]
    nj = pl.cdiv(f, bf)
    assert nj >= 3 and m % bm == 0
    kern = functools.partial(_ffn_kernel, f_total=f, bf=bf, final_norm=final_norm)
    return pl.pallas_call(
        kern,
        grid=(m // bm, nj),
        in_specs=[
            pl.BlockSpec(memory_space=pl.ANY),
            _row_spec(layer, d),
            pl.BlockSpec((None, d, bf), lambda i, j: (layer, 0, j)),
            pl.BlockSpec((None, d, bf), lambda i, j: (layer, 0, j)),
            pl.BlockSpec((None, bf, d), lambda i, j: (layer, j, 0)),
            pl.BlockSpec((1, d), lambda i, j: (0, 0)),
        ],
        out_specs=pl.BlockSpec((bm, d), lambda i, j: (i, 0)),
        out_shape=jax.ShapeDtypeStruct((m, d), F32),
        scratch_shapes=[pltpu.VMEM((bm, d), BF16),
                        pltpu.VMEM((bm, d), F32),
                        pltpu.SemaphoreType.DMA(())],
        compiler_params=_cparams(("arbitrary", "arbitrary")),
        name="ffn_final" if final_norm else "ffn",
    )(x2d, _rows3(norm_g), wg, wu, wd, final_g.reshape(1, d))


def _cached_col_map(layer, first_block, nb):
    return lambda i, j: (layer, 0, first_block + jnp.where(i == 0, j, nb - 1))


def _conv_block(ext_ref, cw_ref, cb_ref, r0, cs):
    rb = ROW_BLOCK
    base = HALO - (CONV_WIDTH - 1)
    e = ext_ref[pl.ds(r0, rb + HALO), cs]
    acc = jnp.broadcast_to(cb_ref[:, cs], (rb, LANES))
    for r in range(SUBLANES):
        qs = [q for q in range((base + CONV_WIDTH - 1) // SUBLANES + 1)
              if 0 <= SUBLANES * q + r - base < CONV_WIDTH]
        s = e if r == 0 else pltpu.roll(e, rb + HALO - r, 0)
        for q in qs:
            k = SUBLANES * q + r - base
            acc = acc + s[SUBLANES * q:SUBLANES * q + rb] * cw_ref[k:k + 1, cs]
    return acc


def _pool_sums(ext_ref, r0, cs):
    rb = ROW_BLOCK
    x = ext_ref[pl.ds(r0, rb + HALO), cs]
    tok = x[HALO:HALO + rb]
    s = x
    start, width = 0, 1
    sums = []
    while width < max(POOL_WINDOWS):
        new_start = start + SUBLANES
        n = rb + HALO - new_start
        off = new_start - start
        s = s[off:off + n] + s[off - width:off - width + n]
        start, width = new_start, 2 * width
        sums.append(s[HALO - start:HALO - start + rb])
    return tok, sums


def _ab_in_kernel(x_hbm, g_ref, wa_ref, wgate_ref, wp_ref, cw_ref, cb_ref, lg_ref, lb_ref,
                  pw_ref, ps_ref, ao_ref, pb_ref,
                  h_ref, exta_ref, extp_ref, halo_a_ref, halo_p_ref, y_ref, pooled_ref,
                  wc_ref, xbuf, sem, *, bm, tiles_per_seq):
    i = pl.program_id(0)
    j = pl.program_id(1)
    nb, _, bn = y_ref.shape
    rb = ROW_BLOCK
    x_arrival = _x_tile_prefetch(x_hbm, xbuf, sem)

    @pl.when(i == 0)
    def _():
        halo_a_ref[j] = jnp.zeros((HALO, bn), F32)
        halo_p_ref[j] = jnp.zeros((HALO, bn), F32)
        wc_ref[0, j] = wa_ref[...].astype(BF16)
        wc_ref[1, j] = wgate_ref[...].astype(BF16)
        wc_ref[2, j] = wp_ref[...].astype(BF16)

    t_in_seq = i % tiles_per_seq
    seq_start = t_in_seq == 0
    window = jnp.left_shift(2, j)

    def body(first, last):
        pw = pw_ref[...].astype(BF16)
        exta_ref[0:HALO, :] = jnp.where(seq_start, 0.0, halo_a_ref[j])
        extp_ref[0:HALO, :] = jnp.where(seq_start, 0.0, halo_p_ref[j])
        if first:
            x_arrival.wait()

        def project(q):
            rows = pl.ds(q * ROW_SLAB, ROW_SLAB)
            if first:
                h_ref[rows, :] = _rms_norm_f32(xbuf[rows, :], g_ref[...]).astype(BF16)
            h = h_ref[rows, :]
            gate = _dot(h, wc_ref[1, j])
            return _dot(h, wc_ref[0, j]) * jax.nn.sigmoid(gate), _dot(h, wc_ref[2, j])

        nslab = bm // ROW_SLAB
        ahead = project(0)
        for q in range(nslab):
            q0 = q * ROW_SLAB
            rows = pl.ds(q0, ROW_SLAB)
            exta_ref[pl.ds(HALO + q0, ROW_SLAB), :] = ahead[0]
            extp_ref[pl.ds(HALO + q0, ROW_SLAB), :] = ahead[1]
            if q + 1 < nslab:
                ahead = project(q + 1)

            for c in range(bn // LANES):
                cs = slice(c * LANES, (c + 1) * LANES)
                for r0 in range(q0, q0 + ROW_SLAB, rb):
                    y_ref[j, pl.ds(r0, rb), cs] = _conv_block(exta_ref, cw_ref, cb_ref, r0, cs)
                    tok, sums = _pool_sums(extp_ref, r0, cs)
                    s = sums[0]
                    for k in range(1, len(sums)):
                        s = jnp.where(j == k, sums[k], s)
                    pos = t_in_seq * bm + r0 + lax.broadcasted_iota(jnp.int32, (rb, 1), 0)
                    cnt = jnp.minimum(pos + 1, window).astype(F32)
                    pooled_ref[pl.ds(r0, rb), cs] = (s / cnt - tok).astype(BF16)
            pb = _dot(pooled_ref[rows, :], pw) * ps_ref[...]
            pb_ref[rows, :] = pb.astype(BF16)

            if last:
                mu, scale = _layer_norm_stats(lambda c: y_ref[c, rows, :], nb, nb * bn)
                for c in range(nb):
                    cs = slice(c * bn, (c + 1) * bn)
                    yn = (y_ref[c, rows, :] - mu) * scale * lg_ref[:, cs] + lb_ref[:, cs]
                    ao_ref[rows, cs] = (yn * jax.nn.sigmoid(yn)).astype(BF16)
        halo_a_ref[j] = exta_ref[bm:bm + HALO, :]
        halo_p_ref[j] = extp_ref[bm:bm + HALO, :]

    pl.when(j == 0)(lambda: body(True, False))
    pl.when((j > 0) & (j < nb - 1))(lambda: body(False, False))
    pl.when(j == nb - 1)(lambda: body(False, True))


def _ab_out_kernel(ao_ref, pb_ref, x_ref, wo_ref, o_ref, woc_ref):
    i = pl.program_id(0)
    n = pl.program_id(1)
    dc = ao_ref.shape[1]

    @pl.when(i == 0)
    def _():
        woc_ref[n] = wo_ref[...].astype(BF16)

    o_ref[...] = (x_ref[...] + _dot(ao_ref[...], woc_ref[n, 0:dc, :])
                  + _dot(pb_ref[...], woc_ref[n, dc:2 * dc, :]))


def _ab_in(x2d, layer, norm_g, li, w_in, conv_w, conv_b, ln_g, ln_b, pool_w, pool_scale, *,
           seq, bm=1024):
    m, d = x2d.shape
    dc = w_in.shape[2] // 3
    bn = dc // len(POOL_WINDOWS)
    nb = dc // bn
    assert seq % bm == 0 and bm % ROW_SLAB == 0 and ROW_SLAB % ROW_BLOCK == 0
    assert bn % LANES == 0 and nb >= 3
    assert HALO >= CONV_WIDTH - 1 and HALO >= max(POOL_WINDOWS)
    assert POOL_WINDOWS == tuple(2 << k for k in range(nb))
    kern = functools.partial(_ab_in_kernel, bm=bm, tiles_per_seq=seq // bm)
    col = lambda first: (lambda i, j: (li, 0, first + j))
    return pl.pallas_call(
        kern,
        grid=(m // bm, nb),
        in_specs=[
            pl.BlockSpec(memory_space=pl.ANY),
            _row_spec(layer, d),
            pl.BlockSpec((None, d, bn), _cached_col_map(li, 0, nb)),
            pl.BlockSpec((None, d, bn), _cached_col_map(li, nb, nb)),
            pl.BlockSpec((None, d, bn), _cached_col_map(li, 2 * nb, nb)),
            pl.BlockSpec((None, CONV_WIDTH, bn), col(0)),
            pl.BlockSpec((None, 1, bn), col(0)),
            _row_spec(li, dc),
            _row_spec(li, dc),
            pl.BlockSpec((None, None, bn, bn), lambda i, j: (li, j, 0, 0)),
            pl.BlockSpec((None, 1, bn), col(0)),
        ],
        out_specs=[pl.BlockSpec((bm, dc), lambda i, j: (i, 0)),
                   pl.BlockSpec((bm, bn), lambda i, j: (i, j))],
        out_shape=[jax.ShapeDtypeStruct((m, dc), BF16),
                   jax.ShapeDtypeStruct((m, dc), BF16)],
        scratch_shapes=[pltpu.VMEM((bm, d), BF16),
                        pltpu.VMEM((bm + HALO, bn), F32),
                        pltpu.VMEM((bm + HALO, bn), F32),
                        pltpu.VMEM((nb, HALO, bn), F32),
                        pltpu.VMEM((nb, HALO, bn), F32),
                        pltpu.VMEM((nb, bm, bn), F32),
                        pltpu.VMEM((bm, bn), BF16),
                        pltpu.VMEM((3, nb, d, bn), BF16),
                        pltpu.VMEM((bm, d), F32),
                        pltpu.SemaphoreType.DMA(())],
        compiler_params=_cparams(("arbitrary", "arbitrary")),
        name="ab_in",
    )(x2d, _rows3(norm_g), w_in, w_in, w_in, conv_w, _rows3(conv_b), _rows3(ln_g),
      _rows3(ln_b), pool_w, _rows3(pool_scale))


def _ab_out(x2d, ao, pb, li, w_out, *, bm=1024, bn=1024):
    m, d = x2d.shape
    dc = ao.shape[1]
    return pl.pallas_call(
        _ab_out_kernel,
        grid=(m // bm, d // bn),
        in_specs=[
            pl.BlockSpec((bm, dc), lambda i, n: (i, 0)),
            pl.BlockSpec((bm, dc), lambda i, n: (i, 0)),
            pl.BlockSpec((bm, bn), lambda i, n: (i, n)),
            pl.BlockSpec((None, 2 * dc, bn), _cached_col_map(li, 0, d // bn)),
        ],
        out_specs=pl.BlockSpec((bm, bn), lambda i, n: (i, n)),
        out_shape=jax.ShapeDtypeStruct((m, d), F32),
        scratch_shapes=[pltpu.VMEM((d // bn, 2 * dc, bn), BF16)],
        compiler_params=_cparams(("arbitrary", "arbitrary")),
        name="ab_out",
    )(ao, pb, x2d, w_out)


def _sgu_in_kernel(x_hbm, g_ref, w_ref, lg_ref, lb_ref, u_ref, vn_ref,
                   h_ref, v_ref, mu_ref, sc_ref, xbuf, sem):
    j = pl.program_id(1)
    nh, _, bn = v_ref.shape
    x_arrival = _x_tile_prefetch(x_hbm, xbuf, sem)

    def z():
        return jax.nn.gelu(_dot(h_ref[...], w_ref[...].astype(BF16)), approximate=True)

    def v_step(first):
        if first:
            x_arrival.wait()
            h_ref[...] = _rms_norm_f32(xbuf[...], g_ref[...]).astype(BF16)
        v_ref[j] = z()

    pl.when(j == 0)(lambda: v_step(True))
    pl.when((j > 0) & (j < nh))(lambda: v_step(False))

    def u_step(c):
        u_ref[...] = z().astype(BF16)
        if c == 0:
            mu_ref[...], sc_ref[...] = _layer_norm_stats(lambda k: v_ref[k], nh, nh * bn)
        cs = slice(c * bn, (c + 1) * bn)
        vn = (v_ref[c] - mu_ref[...]) * sc_ref[...] * lg_ref[:, cs] + lb_ref[:, cs]
        vn_ref[:, cs] = vn.astype(BF16)

    for c in range(nh):
        pl.when(j == nh + c)(functools.partial(u_step, c))


def _sgu_in(x2d, layer, norm_g, li, w_in, ln_g, ln_b, *, bm=1024, bn=1024):
    m, d = x2d.shape
    ds = w_in.shape[2] // 2
    nh = ds // bn
    assert nh >= 2 and m % bm == 0
    return pl.pallas_call(
        _sgu_in_kernel,
        grid=(m // bm, 2 * nh),
        in_specs=[
            pl.BlockSpec(memory_space=pl.ANY),
            _row_spec(layer, d),
            pl.BlockSpec((None, d, bn), lambda i, j: (li, 0, (j + nh) % (2 * nh))),
            _row_spec(li, ds),
            _row_spec(li, ds),
        ],
        out_specs=[pl.BlockSpec((bm, bn), lambda i, j: (i, jnp.maximum(j - nh, 0))),
                   pl.BlockSpec((bm, ds), lambda i, j: (i, 0))],
        out_shape=[jax.ShapeDtypeStruct((m, ds), BF16),
                   jax.ShapeDtypeStruct((m, ds), BF16)],
        scratch_shapes=[pltpu.VMEM((bm, d), BF16),
                        pltpu.VMEM((nh, bm, bn), F32),
                        pltpu.VMEM((bm, 1), F32),
                        pltpu.VMEM((bm, 1), F32),
                        pltpu.VMEM((bm, d), F32),
                        pltpu.SemaphoreType.DMA(())],
        compiler_params=_cparams(("arbitrary", "arbitrary")),
        name="sgu_in",
    )(x2d, _rows3(norm_g), w_in, _rows3(ln_g), _rows3(ln_b))


def _sgu_out_kernel(u_ref, vn_ref, sw_ref, sbt_ref, x_ref, wo_ref, o_ref, y_ref, woc_ref,
                    *, bm):
    n = pl.program_id(1)
    ds = vn_ref.shape[1]
    hd = ds // SGU_HEADS

    @pl.when(pl.program_id(1) == 0)
    def _():
        ri = lax.broadcasted_iota(jnp.int32, (SGU_LEN, SGU_LEN), 0) // CHUNK
        ci = lax.broadcasted_iota(jnp.int32, (SGU_LEN, SGU_LEN), 1) // CHUNK
        mask = ci <= ri
        for g in range(SGU_HEADS):
            wg = jnp.where(mask, sw_ref[g], 0.0).astype(BF16)
            bias = sbt_ref[:, g:g + 1]
            gs = slice(g * hd, (g + 1) * hd)
            for w in range(bm // SGU_LEN):
                rs = pl.ds(w * SGU_LEN, SGU_LEN)
                mixed = _dot(wg, vn_ref[rs, gs]) + bias
                y_ref[rs, gs] = (u_ref[rs, gs].astype(F32) * mixed).astype(BF16)

    @pl.when(pl.program_id(0) == 0)
    def _():
        woc_ref[n] = wo_ref[...].astype(BF16)

    o_ref[...] = x_ref[...] + _dot(y_ref[...], woc_ref[n])


def _sgu_out(x2d, u, vn, li, sgu_w, sgu_b, w_out, *, bm=1024, bn=512):
    m, d = x2d.shape
    ds = vn.shape[1]
    kern = functools.partial(_sgu_out_kernel, bm=bm)
    return pl.pallas_call(
        kern,
        grid=(m // bm, d // bn),
        in_specs=[
            pl.BlockSpec((bm, ds), lambda i, n: (i, 0)),
            pl.BlockSpec((bm, ds), lambda i, n: (i, 0)),
            pl.BlockSpec((None,) + sgu_w.shape[1:], lambda i, n: (li, 0, 0, 0)),
            pl.BlockSpec((None, SGU_LEN, SGU_HEADS), lambda i, n: (li, 0, 0)),
            pl.BlockSpec((bm, bn), lambda i, n: (i, n)),
            pl.BlockSpec((None, ds, bn), _cached_col_map(li, 0, d // bn)),
        ],
        out_specs=pl.BlockSpec((bm, bn), lambda i, n: (i, n)),
        out_shape=jax.ShapeDtypeStruct((m, d), F32),
        scratch_shapes=[pltpu.VMEM((bm, ds), BF16),
                        pltpu.VMEM((d // bn, ds, bn), BF16)],
        compiler_params=_cparams(("arbitrary", "arbitrary")),
        name="sgu_out",
    )(u, vn, sgu_w, jnp.swapaxes(sgu_b, 1, 2), x2d, w_out)


def kernel(x, norm_ffn1, ffn1_w_gate, ffn1_w_up, ffn1_w_down, norm_mix, norm_ffn2,
           ffn2_w_gate, ffn2_w_up, ffn2_w_down, ab_w_in, conv_w, conv_b, conv_ln_g,
           conv_ln_b, pool_w, pool_scale, ab_w_out, c_w_in, sgu_ln_g, sgu_ln_b, sgu_w,
           sgu_b, c_w_out, final_norm):
    bsz, seq, d = x.shape
    depth = norm_ffn1.shape[0]
    x2d = x.reshape(bsz * seq, d)
    for layer in range(depth):
        x2d = _ffn(x2d, layer, norm_ffn1, ffn1_w_gate, ffn1_w_up, ffn1_w_down,
                   final_norm, final_norm=False)
        i = layer // 2
        if layer % 2 == 0:
            ao, pb = _ab_in(x2d, layer, norm_mix, i, ab_w_in, conv_w, conv_b, conv_ln_g,
                            conv_ln_b, pool_w, pool_scale, seq=seq)
            x2d = _ab_out(x2d, ao, pb, i, ab_w_out)
        else:
            u, vn = _sgu_in(x2d, layer, norm_mix, i, c_w_in, sgu_ln_g, sgu_ln_b)
            x2d = _sgu_out(x2d, u, vn, i, sgu_w, sgu_b, c_w_out)
        x2d = _ffn(x2d, layer, norm_ffn2, ffn2_w_gate, ffn2_w_up, ffn2_w_down,
                   final_norm, final_norm=(layer == depth - 1))
    return x2d.reshape(bsz, seq, d)
```

```python
import functools

import jax
import jax.numpy as jnp
from jax import lax
from jax.experimental import pallas as pl
from jax.experimental.pallas import tpu as pltpu

EPS = 1e-6
CHUNK = 64
CONV_WIDTH = 31
POOL_WINDOWS = (2, 4, 8, 16)
SGU_LEN = 128
SGU_HEADS = 8
HALO = 32
SUBLANES = 8
LANES = 128
ROW_BLOCK = 16
ROW_SLAB = 256
BF16 = jnp.bfloat16
F32 = jnp.float32

VMEM_LIMIT_BYTES = 58 * 1024 * 1024


def _cparams(semantics):
    return pltpu.CompilerParams(dimension_semantics=semantics,
                                vmem_limit_bytes=VMEM_LIMIT_BYTES)


def _rms_norm_f32(xf, g):
    ms = jnp.mean(xf * xf, axis=-1, keepdims=True)
    return xf * lax.rsqrt(ms + EPS) * g


def _layer_norm_stats(load, nchunks, width):
    total = load(0).sum(axis=-1, keepdims=True)
    for c in range(1, nchunks):
        total = total + load(c).sum(axis=-1, keepdims=True)
    mu = total / width
    sq = None
    for c in range(nchunks):
        xc = load(c) - mu
        part = (xc * xc).sum(axis=-1, keepdims=True)
        sq = part if sq is None else sq + part
    return mu, lax.rsqrt(sq / width + EPS)


def _dot(a, b):
    return jnp.dot(a, b, preferred_element_type=F32)


def _rows3(a):
    return a.reshape(a.shape[0], 1, a.shape[1])


def _row_spec(layer, c):
    return pl.BlockSpec((None, 1, c), lambda *_: (layer, 0, 0))


def _x_tile_prefetch(x_hbm, xbuf, sem):
    i = pl.program_id(0)
    j = pl.program_id(1)
    bm = xbuf.shape[0]

    def x_copy(tile):
        return pltpu.make_async_copy(x_hbm.at[pl.ds(tile * bm, bm), :], xbuf, sem)

    @pl.when((i == 0) & (j == 0))
    def _():
        x_copy(0).start()

    @pl.when((j == 1) & (i + 1 < pl.num_programs(0)))
    def _():
        x_copy(i + 1).start()

    return x_copy(i)


def _ffn_kernel(x_hbm, g_ref, wg_ref, wu_ref, wd_ref, fg_ref, o_ref, h_ref, xbuf, sem, *,
                f_total, bf, final_norm):
    j = pl.program_id(1)
    nj = pl.num_programs(1)
    x_arrival = _x_tile_prefetch(x_hbm, xbuf, sem)

    def step(width, first=False, last=False):
        if first:
            x_arrival.wait()
            h_ref[...] = _rms_norm_f32(xbuf[...], g_ref[...]).astype(BF16)
        h = h_ref[...]

        def swiglu(g, u):
            return (g * jax.nn.sigmoid(g) * u * 0.5).astype(BF16)

        g = _dot(h, wg_ref[:, :width].astype(BF16))
        u = _dot(h, wu_ref[:, :width].astype(BF16))
        down = _dot(swiglu(g, u), wd_ref[:width, :].astype(BF16))
        if first:
            o_ref[...] = xbuf[...] + down
        else:
            o_ref[...] += down
        if last and final_norm:
            o_ref[...] = _rms_norm_f32(o_ref[...], fg_ref[...])

    rem = f_total % bf
    pl.when(j == 0)(lambda: step(bf, first=True))
    pl.when((j > 0) & (j < nj - 1))(lambda: step(bf))
    pl.when(j == nj - 1)(lambda: step(rem if rem else bf, last=True))


def _ffn(x2d, layer, norm_g, wg, wu, wd, final_g, *, final_norm, bm=1024, bf=512):
    m, d = x2d.shape
    f = wg.shape[2]
    nj = pl.cdiv(f, bf)
    assert nj >= 3 and m % bm == 0
    kern = functools.partial(_ffn_kernel, f_total=f, bf=bf, final_norm=final_norm)
    return pl.pallas_call(
        kern,
        grid=(m // bm, nj),
        in_specs=[
            pl.BlockSpec(memory_space=pl.ANY),
            _row_spec(layer, d),
            pl.BlockSpec((None, d, bf), lambda i, j: (layer, 0, j)),
            pl.BlockSpec((None, d, bf), lambda i, j: (layer, 0, j)),
            pl.BlockSpec((None, bf, d), lambda i, j: (layer, j, 0)),
            pl.BlockSpec((1, d), lambda i, j: (0, 0)),
        ],
        out_specs=pl.BlockSpec((bm, d), lambda i, j: (i, 0)),
        out_shape=jax.ShapeDtypeStruct((m, d), F32),
        scratch_shapes=[pltpu.VMEM((bm, d), BF16),
                        pltpu.VMEM((bm, d), F32),
                        pltpu.SemaphoreType.DMA(())],
        compiler_params=_cparams(("arbitrary", "arbitrary")),
        name="ffn_final" if final_norm else "ffn",
    )(x2d, _rows3(norm_g), wg, wu, wd, final_g.reshape(1, d))


def _cached_col_map(layer, first_block, nb):
    return lambda i, j: (layer, 0, first_block + jnp.where(i == 0, j, nb - 1))


def _conv_block(ext_ref, cw_ref, cb_ref, r0, cs):
    rb = ROW_BLOCK
    base = HALO - (CONV_WIDTH - 1)
    e = ext_ref[pl.ds(r0, rb + HALO), cs]
    acc = jnp.broadcast_to(cb_ref[:, cs], (rb, LANES))
    for r in range(SUBLANES):
        qs = [q for q in range((base + CONV_WIDTH - 1) // SUBLANES + 1)
              if 0 <= SUBLANES * q + r - base < CONV_WIDTH]
        s = e if r == 0 else pltpu.roll(e, rb + HALO - r, 0)
        for q in qs:
            k = SUBLANES * q + r - base
            acc = acc + s[SUBLANES * q:SUBLANES * q + rb] * cw_ref[k:k + 1, cs]
    return acc


def _pool_sums(ext_ref, r0, cs):
    rb = ROW_BLOCK
    x = ext_ref[pl.ds(r0, rb + HALO), cs]
    tok = x[HALO:HALO + rb]
    s = x
    start, width = 0, 1
    sums = []
    while width < max(POOL_WINDOWS):
        new_start = start + SUBLANES
        n = rb + HALO - new_start
        off = new_start - start
        s = s[off:off + n] + s[off - width:off - width + n]
        start, width = new_start, 2 * width
        sums.append(s[HALO - start:HALO - start + rb])
    return tok, sums


def _ab_in_kernel(x_hbm, g_ref, wa_ref, wgate_ref, wp_ref, cw_ref, cb_ref, lg_ref, lb_ref,
                  pw_ref, ps_ref, ao_ref, pb_ref,
                  h_ref, exta_ref, extp_ref, halo_a_ref, halo_p_ref, y_ref, pooled_ref,
                  wc_ref, xbuf, sem, *, bm, tiles_per_seq):
    i = pl.program_id(0)
    j = pl.program_id(1)
    nb, _, bn = y_ref.shape
    rb = ROW_BLOCK
    x_arrival = _x_tile_prefetch(x_hbm, xbuf, sem)

    @pl.when(i == 0)
    def _():
        halo_a_ref[j] = jnp.zeros((HALO, bn), F32)
        halo_p_ref[j] = jnp.zeros((HALO, bn), F32)
        wc_ref[0, j] = wa_ref[...].astype(BF16)
        wc_ref[1, j] = wgate_ref[...].astype(BF16)
        wc_ref[2, j] = wp_ref[...].astype(BF16)

    t_in_seq = i % tiles_per_seq
    seq_start = t_in_seq == 0
    window = jnp.left_shift(2, j)

    def body(first, last):
        pw = pw_ref[...].astype(BF16)
        exta_ref[0:HALO, :] = jnp.where(seq_start, 0.0, halo_a_ref[j])
        extp_ref[0:HALO, :] = jnp.where(seq_start, 0.0, halo_p_ref[j])
        if first:
            x_arrival.wait()

        def project(q):
            rows = pl.ds(q * ROW_SLAB, ROW_SLAB)
            if first:
                h_ref[rows, :] = _rms_norm_f32(xbuf[rows, :], g_ref[...]).astype(BF16)
            h = h_ref[rows, :]
            gate = _dot(h, wc_ref[1, j])
            return _dot(h, wc_ref[0, j]) * jax.nn.sigmoid(gate), _dot(h, wc_ref[2, j])

        nslab = bm // ROW_SLAB
        ahead = project(0)
        for q in range(nslab):
            q0 = q * ROW_SLAB
            rows = pl.ds(q0, ROW_SLAB)
            exta_ref[pl.ds(HALO + q0, ROW_SLAB), :] = ahead[0]
            extp_ref[pl.ds(HALO + q0, ROW_SLAB), :] = ahead[1]
            if q + 1 < nslab:
                ahead = project(q + 1)

            for c in range(bn // LANES):
                cs = slice(c * LANES, (c + 1) * LANES)
                for r0 in range(q0, q0 + ROW_SLAB, rb):
                    y_ref[j, pl.ds(r0, rb), cs] = _conv_block(exta_ref, cw_ref, cb_ref, r0, cs)
                    tok, sums = _pool_sums(extp_ref, r0, cs)
                    s = sums[0]
                    for k in range(1, len(sums)):
                        s = jnp.where(j == k, sums[k], s)
                    pos = t_in_seq * bm + r0 + lax.broadcasted_iota(jnp.int32, (rb, 1), 0)
                    cnt = jnp.minimum(pos + 1, window).astype(F32)
                    pooled_ref[pl.ds(r0, rb), cs] = (s / cnt - tok).astype(BF16)
            pb = _dot(pooled_ref[rows, :], pw) * ps_ref[...]
            pb_ref[rows, :] = pb.astype(BF16)

            if last:
                mu, scale = _layer_norm_stats(lambda c: y_ref[c, rows, :], nb, nb * bn)
                for c in range(nb):
                    cs = slice(c * bn, (c + 1) * bn)
                    yn = (y_ref[c, rows, :] - mu) * scale * lg_ref[:, cs] + lb_ref[:, cs]
                    ao_ref[rows, cs] = (yn * jax.nn.sigmoid(yn)).astype(BF16)
        halo_a_ref[j] = exta_ref[bm:bm + HALO, :]
        halo_p_ref[j] = extp_ref[bm:bm + HALO, :]

    pl.when(j == 0)(lambda: body(True, False))
    pl.when((j > 0) & (j < nb - 1))(lambda: body(False, False))
    pl.when(j == nb - 1)(lambda: body(False, True))


def _ab_out_kernel(ao_ref, pb_ref, x_ref, wo_ref, o_ref, woc_ref):
    i = pl.program_id(0)
    n = pl.program_id(1)
    dc = ao_ref.shape[1]

    @pl.when(i == 0)
    def _():
        woc_ref[n] = wo_ref[...].astype(BF16)

    o_ref[...] = (x_ref[...] + _dot(ao_ref[...], woc_ref[n, 0:dc, :])
                  + _dot(pb_ref[...], woc_ref[n, dc:2 * dc, :]))


def _ab_in(x2d, layer, norm_g, li, w_in, conv_w, conv_b, ln_g, ln_b, pool_w, pool_scale, *,
           seq, bm=1024):
    m, d = x2d.shape
    dc = w_in.shape[2] // 3
    bn = dc // len(POOL_WINDOWS)
    nb = dc // bn
    assert seq % bm == 0 and bm % ROW_SLAB == 0 and ROW_SLAB % ROW_BLOCK == 0
    assert bn % LANES == 0 and nb >= 3
    assert HALO >= CONV_WIDTH - 1 and HALO >= max(POOL_WINDOWS)
    assert POOL_WINDOWS == tuple(2 << k for k in range(nb))
    kern = functools.partial(_ab_in_kernel, bm=bm, tiles_per_seq=seq // bm)
    col = lambda first: (lambda i, j: (li, 0, first + j))
    return pl.pallas_call(
        kern,
        grid=(m // bm, nb),
        in_specs=[
            pl.BlockSpec(memory_space=pl.ANY),
            _row_spec(layer, d),
            pl.BlockSpec((None, d, bn), _cached_col_map(li, 0, nb)),
            pl.BlockSpec((None, d, bn), _cached_col_map(li, nb, nb)),
            pl.BlockSpec((None, d, bn), _cached_col_map(li, 2 * nb, nb)),
            pl.BlockSpec((None, CONV_WIDTH, bn), col(0)),
            pl.BlockSpec((None, 1, bn), col(0)),
            _row_spec(li, dc),
            _row_spec(li, dc),
            pl.BlockSpec((None, None, bn, bn), lambda i, j: (li, j, 0, 0)),
            pl.BlockSpec((None, 1, bn), col(0)),
        ],
        out_specs=[pl.BlockSpec((bm, dc), lambda i, j: (i, 0)),
                   pl.BlockSpec((bm, bn), lambda i, j: (i, j))],
        out_shape=[jax.ShapeDtypeStruct((m, dc), BF16),
                   jax.ShapeDtypeStruct((m, dc), BF16)],
        scratch_shapes=[pltpu.VMEM((bm, d), BF16),
                        pltpu.VMEM((bm + HALO, bn), F32),
                        pltpu.VMEM((bm + HALO, bn), F32),
                        pltpu.VMEM((nb, HALO, bn), F32),
                        pltpu.VMEM((nb, HALO, bn), F32),
                        pltpu.VMEM((nb, bm, bn), F32),
                        pltpu.VMEM((bm, bn), BF16),
                        pltpu.VMEM((3, nb, d, bn), BF16),
                        pltpu.VMEM((bm, d), F32),
                        pltpu.SemaphoreType.DMA(())],
        compiler_params=_cparams(("arbitrary", "arbitrary")),
        name="ab_in",
    )(x2d, _rows3(norm_g), w_in, w_in, w_in, conv_w, _rows3(conv_b), _rows3(ln_g),
      _rows3(ln_b), pool_w, _rows3(pool_scale))


def _ab_out(x2d, ao, pb, li, w_out, *, bm=1024, bn=1024):
    m, d = x2d.shape
    dc = ao.shape[1]
    return pl.pallas_call(
        _ab_out_kernel,
        grid=(m // bm, d // bn),
        in_specs=[
            pl.BlockSpec((bm, dc), lambda i, n: (i, 0)),
            pl.BlockSpec((bm, dc), lambda i, n: (i, 0)),
            pl.BlockSpec((bm, bn), lambda i, n: (i, n)),
            pl.BlockSpec((None, 2 * dc, bn), _cached_col_map(li, 0, d // bn)),
        ],
        out_specs=pl.BlockSpec((bm, bn), lambda i, n: (i, n)),
        out_shape=jax.ShapeDtypeStruct((m, d), F32),
        scratch_shapes=[pltpu.VMEM((d // bn, 2 * dc, bn), BF16)],
        compiler_params=_cparams(("arbitrary", "arbitrary")),
        name="ab_out",
    )(ao, pb, x2d, w_out)


def _sgu_in_kernel(x_hbm, g_ref, w_ref, lg_ref, lb_ref, u_ref, vn_ref,
                   h_ref, v_ref, mu_ref, sc_ref, xbuf, sem):
    j = pl.program_id(1)
    nh, _, bn = v_ref.shape
    x_arrival = _x_tile_prefetch(x_hbm, xbuf, sem)

    def z():
        return jax.nn.gelu(_dot(h_ref[...], w_ref[...].astype(BF16)), approximate=True)

    def v_step(first):
        if first:
            x_arrival.wait()
            h_ref[...] = _rms_norm_f32(xbuf[...], g_ref[...]).astype(BF16)
        v_ref[j] = z()

    pl.when(j == 0)(lambda: v_step(True))
    pl.when((j > 0) & (j < nh))(lambda: v_step(False))

    def u_step(c):
        u_ref[...] = z().astype(BF16)
        if c == 0:
            mu_ref[...], sc_ref[...] = _layer_norm_stats(lambda k: v_ref[k], nh, nh * bn)
        cs = slice(c * bn, (c + 1) * bn)
        vn = (v_ref[c] - mu_ref[...]) * sc_ref[...] * lg_ref[:, cs] + lb_ref[:, cs]
        vn_ref[:, cs] = vn.astype(BF16)

    for c in range(nh):
        pl.when(j == nh + c)(functools.partial(u_step, c))


def _sgu_in(x2d, layer, norm_g, li, w_in, ln_g, ln_b, *, bm=1024, bn=1024):
    m, d = x2d.shape
    ds = w_in.shape[2] // 2
    nh = ds // bn
    assert nh >= 2 and m % bm == 0
    return pl.pallas_call(
        _sgu_in_kernel,
        grid=(m // bm, 2 * nh),
        in_specs=[
            pl.BlockSpec(memory_space=pl.ANY),
            _row_spec(layer, d),
            pl.BlockSpec((None, d, bn), lambda i, j: (li, 0, (j + nh) % (2 * nh))),
            _row_spec(li, ds),
            _row_spec(li, ds),
        ],
        out_specs=[pl.BlockSpec((bm, bn), lambda i, j: (i, jnp.maximum(j - nh, 0))),
                   pl.BlockSpec((bm, ds), lambda i, j: (i, 0))],
        out_shape=[jax.ShapeDtypeStruct((m, ds), BF16),
                   jax.ShapeDtypeStruct((m, ds), BF16)],
        scratch_shapes=[pltpu.VMEM((bm, d), BF16),
                        pltpu.VMEM((nh, bm, bn), F32),
                        pltpu.VMEM((bm, 1), F32),
                        pltpu.VMEM((bm, 1), F32),
                        pltpu.VMEM((bm, d), F32),
                        pltpu.SemaphoreType.DMA(())],
        compiler_params=_cparams(("arbitrary", "arbitrary")),
        name="sgu_in",
    )(x2d, _rows3(norm_g), w_in, _rows3(ln_g), _rows3(ln_b))


def _sgu_out_kernel(u_ref, vn_ref, sw_ref, sbt_ref, x_ref, wo_ref, o_ref, y_ref, woc_ref,
                    *, bm):
    n = pl.program_id(1)
    ds = vn_ref.shape[1]
    hd = ds // SGU_HEADS

    @pl.when(pl.program_id(1) == 0)
    def _():
        ri = lax.broadcasted_iota(jnp.int32, (SGU_LEN, SGU_LEN), 0) // CHUNK
        ci = lax.broadcasted_iota(jnp.int32, (SGU_LEN, SGU_LEN), 1) // CHUNK
        mask = ci <= ri
        for g in range(SGU_HEADS):
            wg = jnp.where(mask, sw_ref[g], 0.0).astype(BF16)
            bias = sbt_ref[:, g:g + 1]
            gs = slice(g * hd, (g + 1) * hd)
            for w in range(bm // SGU_LEN):
                rs = pl.ds(w * SGU_LEN, SGU_LEN)
                mixed = _dot(wg, vn_ref[rs, gs]) + bias
                y_ref[rs, gs] = (u_ref[rs, gs].astype(F32) * mixed).astype(BF16)

    @pl.when(pl.program_id(0) == 0)
    def _():
        woc_ref[n] = wo_ref[...].astype(BF16)

    o_ref[...] = x_ref[...] + _dot(y_ref[...], woc_ref[n])


def _sgu_out(x2d, u, vn, li, sgu_w, sgu_b, w_out, *, bm=1024, bn=512):
    m, d = x2d.shape
    ds = vn.shape[1]
    kern = functools.partial(_sgu_out_kernel, bm=bm)
    return pl.pallas_call(
        kern,
        grid=(m // bm, d // bn),
        in_specs=[
            pl.BlockSpec((bm, ds), lambda i, n: (i, 0)),
            pl.BlockSpec((bm, ds), lambda i, n: (i, 0)),
            pl.BlockSpec((None,) + sgu_w.shape[1:], lambda i, n: (li, 0, 0, 0)),
            pl.BlockSpec((None, SGU_LEN, SGU_HEADS), lambda i, n: (li, 0, 0)),
            pl.BlockSpec((bm, bn), lambda i, n: (i, n)),
            pl.BlockSpec((None, ds, bn), _cached_col_map(li, 0, d // bn)),
        ],
        out_specs=pl.BlockSpec((bm, bn), lambda i, n: (i, n)),
        out_shape=jax.ShapeDtypeStruct((m, d), F32),
        scratch_shapes=[pltpu.VMEM((bm, ds), BF16),
                        pltpu.VMEM((d // bn, ds, bn), BF16)],
        compiler_params=_cparams(("arbitrary", "arbitrary")),
        name="sgu_out",
    )(u, vn, sgu_w, jnp.swapaxes(sgu_b, 1, 2), x2d, w_out)


def kernel(x, norm_ffn1, ffn1_w_gate, ffn1_w_up, ffn1_w_down, norm_mix, norm_ffn2,
           ffn2_w_gate, ffn2_w_up, ffn2_w_down, ab_w_in, conv_w, conv_b, conv_ln_g,
           conv_ln_b, pool_w, pool_scale, ab_w_out, c_w_in, sgu_ln_g, sgu_ln_b, sgu_w,
           sgu_b, c_w_out, final_norm):
    bsz, seq, d = x.shape
    depth = norm_ffn1.shape[0]
    x2d = x.reshape(bsz * seq, d)
    for layer in range(depth):
        x2d = _ffn(x2d, layer, norm_ffn1, ffn1_w_gate, ffn1_w_up, ffn1_w_down,
                   final_norm, final_norm=False)
        i = layer // 2
        if layer % 2 == 0:
            ao, pb = _ab_in(x2d, layer, norm_mix, i, ab_w_in, conv_w, conv_b, conv_ln_g,
                            conv_ln_b, pool_w, pool_scale, seq=seq)
            x2d = _ab_out(x2d, ao, pb, i, ab_w_out)
        else:
            u, vn = _sgu_in(x2d, layer, norm_mix, i, c_w_in, sgu_ln_g, sgu_ln_b)
            x2d = _sgu_out(x2d, u, vn, i, sgu_w, sgu_b, c_w_out)
        x2d = _ffn(x2d, layer, norm_ffn2, ffn2_w_gate, ffn2_w_up, ffn2_w_down,
                   final_norm, final_norm=(layer == depth - 1))
    return x2d.reshape(bsz, seq, d)
```
